```python
import math
import jax, jax.numpy as jnp
from jax import lax
import numpy as np

D_MODEL = 2048
BATCH = 32
SEQ = 256
DEPTH = 1
DEC_BATCH = 8
DEC_SEQ = 4096
PAST_LEN = 512

GRID_W = 64
MIX_W = D_MODEL
HY_W = MIX_W // 2
RET_W = MIX_W - HY_W
RET_HEADS = 8
RET_DK = RET_W // RET_HEADS
RET_CHUNK = 128
SHORT_CONV = 3
HY_ORDER = 2
HY_EMB = 33
HY_FH = 64
HY_TARGET = 1e-2
HY_FAST = 0.3
HY_SLOW = 1.5
PROJ_W = (HY_ORDER + 1) * HY_W + 4 * RET_W
N_EXPERTS = 64
TOP_K = 8
EXPERT_FF = D_MODEL // 4
SHARED_FF = EXPERT_FF
ROUTED_SCALE = 2.5
MOE_BLOCK = 128
ROPE_BASE = 10000.0
LN_EPS = 1e-5
GN_EPS = 1e-6
DN_ALPHA = (2.0 * DEPTH) ** 0.25
DN_BETA = (8.0 * DEPTH) ** -0.25

kernel_name = 'hymba_hyena_retnet_moe_diffusion_step'


def _normal(key, shape, scale):
    return jax.random.normal(key, shape, jnp.float32) * scale


def _layernorm(x, g, b):
    xf = x.astype(jnp.float32)
    mu = jnp.mean(xf, axis=-1, keepdims=True)
    var = jnp.mean(jnp.square(xf - mu), axis=-1, keepdims=True)
    y = (xf - mu) * lax.rsqrt(var + LN_EPS) * g.astype(jnp.float32) + b.astype(jnp.float32)
    return y.astype(x.dtype)


def _short_conv(u, w, b):
    L = u.shape[1]
    up = jnp.pad(u, ((0, 0), (1, 1), (0, 0)))
    return up[:, :L] * w[0] + up[:, 1:L + 1] * w[1] + up[:, 2:L + 2] * w[2] + b


def _hyena_filters_fft(L, w1, b1, w2, b2, w3, freq):
    f32 = jnp.float32
    t = jnp.arange(L, dtype=f32)
    t_norm = t / (L - 1)
    bands = (HY_EMB - 1) // 2
    f = jnp.linspace(1e-4, bands - 1, bands, dtype=f32)
    ang = (2.0 * math.pi / L) * t[:, None] * f[None, :]
    z = jnp.concatenate([t_norm[:, None], jnp.cos(ang), -jnp.sin(ang)], axis=-1)
    fr = freq.astype(f32)
    h = jnp.sin(fr * (z @ w1.astype(f32) + b1.astype(f32)))
    h = jnp.sin(fr * (h @ w2.astype(f32) + b2.astype(f32)))
    h = (h @ w3.astype(f32)).reshape(L, HY_ORDER, 2, HY_W)
    deltas = jnp.linspace(math.log(HY_TARGET) / HY_SLOW, math.log(HY_TARGET) / HY_FAST, HY_W, dtype=f32)
    decay = jnp.exp(-t_norm[:, None] * jnp.abs(deltas)[None, :])
    h = h * decay[:, None, None, :]
    h_f = h[:, :, 0]
    h_b = h[:, :, 1]
    filt = jnp.concatenate([h_f, jnp.zeros((1, HY_ORDER, HY_W), f32), h_b[1:][::-1]], axis=0)
    return jnp.fft.rfft(filt, axis=0)


def _fftconv(u, kf, bias):
    L = u.shape[1]
    U = jnp.fft.rfft(u, n=2 * L, axis=1)
    y = jnp.fft.irfft(U * kf[None], n=2 * L, axis=1)[:, :L]
    return y + u * bias


def _rope_2d(x):
    L = x.shape[1]
    rows = L // GRID_W
    row = jnp.repeat(jnp.arange(rows), GRID_W).astype(jnp.float32)
    col = jnp.tile(jnp.arange(GRID_W), rows).astype(jnp.float32)
    half = RET_DK // 2
    quarter = half // 2
    inv = ROPE_BASE ** (-jnp.arange(quarter, dtype=jnp.float32) / quarter)

    def rot(xh, pos):
        ang = pos[:, None] * inv[None, :]
        cs = jnp.cos(ang)[None, :, None, :]
        sn = jnp.sin(ang)[None, :, None, :]
        a, b = xh[..., :quarter], xh[..., quarter:]
        return jnp.concatenate([a * cs - b * sn, a * sn + b * cs], axis=-1)

    return jnp.concatenate([rot(x[..., :half], row), rot(x[..., half:], col)], axis=-1)


def _retention_scan(q, k, v, log_gamma, s0):
    B, L, H, dk = q.shape
    C = RET_CHUNK
    n = L // C

    def chunks(a):
        return a.reshape(B, n, C, H, a.shape[-1]).transpose(1, 0, 3, 2, 4)

    pos = jnp.arange(C, dtype=jnp.float32)
    diff = pos[:, None] - pos[None, :]
    intra = jnp.where(diff >= 0, jnp.exp(log_gamma[:, None, None] * jnp.maximum(diff, 0.0)), 0.0)
    q_dec = jnp.exp(log_gamma[:, None] * (pos + 1.0))[:, :, None]
    k_dec = jnp.exp(log_gamma[:, None] * (C - 1.0 - pos))[:, :, None]
    c_dec = jnp.exp(log_gamma * C)[:, None, None]

    def step(S, qkv):
        qi, ki, vi = qkv
        att = jnp.einsum('bhnd,bhmd->bhnm', qi, ki) * intra
        o = jnp.einsum('bhnm,bhmv->bhnv', att, vi) + jnp.einsum('bhnd,bhdv->bhnv', qi * q_dec, S)
        S = c_dec * S + jnp.einsum('bhmd,bhmv->bhdv', ki * k_dec, vi)
        return S, o

    S, o = lax.scan(step, s0, (chunks(q), chunks(k), chunks(v)))
    o = o.transpose(1, 0, 3, 2, 4).reshape(B, L, H, v.shape[-1])
    return o, S


def _retention_bidir(q, k, v, lg_f, lg_b, s0_f, s0_b):
    o_f, s_f = _retention_scan(q, k, v, lg_f, s0_f)
    o_b, s_b = _retention_scan(q[:, ::-1], k[:, ::-1], v[:, ::-1], lg_b, s0_b)
    return o_f + o_b[:, ::-1], s_f, s_b


def _swiglu(x, wg, wu, wd):
    return (jax.nn.silu(x @ wg) * (x @ wu)) @ wd


def _routed_experts(xt, idx, gates, w_gate, w_up, w_down):
    T, D = xt.shape
    A = T * TOP_K
    e_flat = idx.reshape(A)
    tok_flat = jnp.repeat(jnp.arange(T, dtype=jnp.int32), TOP_K)
    w_flat = gates.reshape(A)
    order = jnp.argsort(e_flat)
    e_sorted = e_flat[order]
    counts = jnp.bincount(e_flat, length=N_EXPERTS)
    padded = (counts + MOE_BLOCK - 1) // MOE_BLOCK * MOE_BLOCK
    start = jnp.cumsum(counts) - counts
    pend = jnp.cumsum(padded)
    pstart = pend - padded
    dest = pstart[e_sorted] + jnp.arange(A, dtype=jnp.int32) - start[e_sorted]
    P = A + N_EXPERTS * MOE_BLOCK
    nb = P // MOE_BLOCK
    buf_tok = jnp.full((P,), T, jnp.int32).at[dest].set(tok_flat[order])
    buf_w = jnp.zeros((P,), gates.dtype).at[dest].set(w_flat[order])
    blk_e = jnp.clip(jnp.searchsorted(pend, jnp.arange(nb, dtype=jnp.int32) * MOE_BLOCK, side='right'),
                     0, N_EXPERTS - 1)
    x_pad = jnp.concatenate([xt, jnp.zeros((1, D), xt.dtype)], axis=0)

    def step(acc, blk):
        tok, w, e = blk
        y = _swiglu(x_pad[tok], w_gate[e], w_up[e], w_down[e])
        return acc.at[tok].add(y * w[:, None]), None

    acc, _ = lax.scan(step, jnp.zeros((T + 1, D), xt.dtype),
                      (buf_tok.reshape(nb, MOE_BLOCK), buf_w.reshape(nb, MOE_BLOCK), blk_e))
    return acc[:T]


def _moe(h, router_w, router_b, exp_w_gate, exp_w_up, exp_w_down, sh_w_gate, sh_w_up, sh_w_down):
    B, L, D = h.shape
    xt = h.reshape(B * L, D)
    scores = jax.nn.sigmoid((xt @ router_w).astype(jnp.float32))
    _, idx = lax.top_k(scores + router_b.astype(jnp.float32), TOP_K)
    s_sel = jnp.take_along_axis(scores, idx, axis=1)
    gates = ROUTED_SCALE * s_sel / jnp.sum(s_sel, axis=-1, keepdims=True)
    routed = _routed_experts(xt, idx, gates.astype(h.dtype), exp_w_gate, exp_w_up, exp_w_down)
    shared = _swiglu(xt, sh_w_gate, sh_w_up, sh_w_down)
    return (routed + shared).reshape(B, L, D)


def _layer(x, cond, s0_f, s0_b, latent, lp):
    f32 = jnp.float32
    dt = x.dtype
    B, L, _ = x.shape
    mods = jax.nn.silu(cond) @ lp['w_ada'] + lp['b_ada']
    sh1, sc1, g1, sh2, sc2, g2 = jnp.split(mods[:, None, :], 6, axis=-1)
    h = x * (1 + sc1) + sh1
    proj = h @ lp['w_in']

    hy = _short_conv(proj[..., :(HY_ORDER + 1) * HY_W], lp['hy_conv_w'], lp['hy_conv_b']).astype(f32)
    hy_parts = jnp.split(hy, HY_ORDER + 1, axis=-1)
    kf = _hyena_filters_fft(L, lp['hy_ffn_w1'], lp['hy_ffn_b1'], lp['hy_ffn_w2'], lp['hy_ffn_b2'],
                            lp['hy_ffn_w3'], lp['hy_sin_freq'])
    bias = lp['hy_bias'].astype(f32)
    z = hy_parts[0]
    for o in range(HY_ORDER):
        z = hy_parts[o + 1] * _fftconv(z, kf[:, o], bias[o])
    y_hy = z

    q, k, vr, g = jnp.split(proj[..., (HY_ORDER + 1) * HY_W:].astype(f32), 4, axis=-1)
    q = q.reshape(B, L, RET_HEADS, RET_DK)
    k = k.reshape(B, L, RET_HEADS, RET_DK) * (RET_DK ** -0.5)
    vr = vr.reshape(B, L, RET_HEADS, RET_DK)
    if latent:
        q = _rope_2d(q)
        k = _rope_2d(k)
    lg_f = jax.nn.log_sigmoid(lp['ret_decay_f'].astype(f32))
    lg_b = jax.nn.log_sigmoid(lp['ret_decay_b'].astype(f32))
    o, s_f, s_b = _retention_bidir(q, k, vr, lg_f, lg_b, s0_f.astype(f32), s0_b.astype(f32))
    mu = jnp.mean(o, axis=-1, keepdims=True)
    var = jnp.mean(jnp.square(o - mu), axis=-1, keepdims=True)
    o = (o - mu) * lax.rsqrt(var + GN_EPS)
    y_ret = o.reshape(B, L, RET_W) * jax.nn.silu(g)

    mix = jnp.concatenate([y_hy, y_ret], axis=-1).astype(dt) @ lp['w_out']
    x = _layernorm(DN_ALPHA * x + g1 * mix, lp['ln1_g'], lp['ln1_b'])
    h2 = x * (1 + sc2) + sh2
    ffn = _moe(h2, lp['router_w'], lp['router_b'], lp['exp_w_gate'], lp['exp_w_up'], lp['exp_w_down'],
               lp['sh_w_gate'], lp['sh_w_up'], lp['sh_w_down'])
    x = _layernorm(DN_ALPHA * x + g2 * ffn, lp['ln2_g'], lp['ln2_b'])
    return x, s_f, s_b


def setup_inputs(seed: int = 0) -> dict:
    key = jax.random.key(seed)
    ks = jax.random.split(key, 34)
    D = D_MODEL
    E = N_EXPERTS
    F = EXPERT_FF
    FS = SHARED_FF
    dec_base = jnp.asarray(np.log(2.0 ** (5.0 + np.arange(RET_HEADS)) - 1.0), jnp.float32)
    return {
        'x_prompt': _normal(ks[0], (BATCH, SEQ, D), 1.0),
        'x_sample': _normal(ks[1], (DEC_BATCH, DEC_SEQ, D), 1.0),
        'state_ret_fwd': _normal(ks[2], (DEC_BATCH, DEPTH, RET_HEADS, RET_DK, RET_DK), 1.0),
        'state_ret_bwd': _normal(ks[3], (DEC_BATCH, DEPTH, RET_HEADS, RET_DK, RET_DK), 1.0),
        'c': _normal(ks[4], (DEC_BATCH, D), 1.0),
        'c_ctx': _normal(ks[5], (D,), 1.0),
        'w_in': _normal(ks[6], (DEPTH, D, PROJ_W), D ** -0.5),
        'hy_conv_w': _normal(ks[7], (DEPTH, SHORT_CONV, (HY_ORDER + 1) * HY_W), SHORT_CONV ** -0.5),
        'hy_conv_b': _normal(ks[8], (DEPTH, (HY_ORDER + 1) * HY_W), 0.01),
        'hy_ffn_w1': _normal(ks[9], (DEPTH, HY_EMB, HY_FH), HY_EMB ** -0.5),
        'hy_ffn_b1': _normal(ks[10], (DEPTH, HY_FH), 0.1),
        'hy_ffn_w2': _normal(ks[11], (DEPTH, HY_FH, HY_FH), HY_FH ** -0.5),
        'hy_ffn_b2': _normal(ks[12], (DEPTH, HY_FH), 0.1),
        'hy_ffn_w3': _normal(ks[13], (DEPTH, HY_FH, HY_ORDER * 2 * HY_W), 0.1 * HY_FH ** -0.5),
        'hy_sin_freq': 1.0 + _normal(ks[14], (DEPTH, HY_FH), 0.01),
        'hy_bias': _normal(ks[15], (DEPTH, HY_ORDER, HY_W), 0.5),
        'ret_decay_f': dec_base[None, :] + _normal(ks[16], (DEPTH, RET_HEADS), 0.05),
        'ret_decay_b': dec_base[None, :] + _normal(ks[17], (DEPTH, RET_HEADS), 0.05),
        'w_out': _normal(ks[18], (DEPTH, MIX_W, D), DN_BETA * MIX_W ** -0.5),
        'w_ada': _normal(ks[19], (DEPTH, D, 6 * D), 0.5 * D ** -0.5),
        'b_ada': _normal(ks[20], (DEPTH, 6 * D), 0.02),
        'ln1_g': 1.0 + _normal(ks[21], (DEPTH, D), 0.01),
        'ln1_b': _normal(ks[22], (DEPTH, D), 0.01),
        'ln2_g': 1.0 + _normal(ks[23], (DEPTH, D), 0.01),
        'ln2_b': _normal(ks[24], (DEPTH, D), 0.01),
        'router_w': _normal(ks[25], (DEPTH, D, E), D ** -0.5),
        'router_b': _normal(ks[26], (DEPTH, E), 0.01),
        'exp_w_gate': _normal(ks[27], (DEPTH, E, D, F), D ** -0.5),
        'exp_w_up': _normal(ks[28], (DEPTH, E, D, F), D ** -0.5),
        'exp_w_down': _normal(ks[29], (DEPTH, E, F, D), DN_BETA * F ** -0.5),
        'sh_w_gate': _normal(ks[30], (DEPTH, D, FS), D ** -0.5),
        'sh_w_up': _normal(ks[31], (DEPTH, D, FS), D ** -0.5),
        'sh_w_down': _normal(ks[32], (DEPTH, FS, D), DN_BETA * FS ** -0.5),
    }


def reference(x_prompt, x_sample, state_ret_fwd, state_ret_bwd, c, c_ctx, w_in, hy_conv_w, hy_conv_b,
              hy_ffn_w1, hy_ffn_b1, hy_ffn_w2, hy_ffn_b2, hy_ffn_w3, hy_sin_freq, hy_bias,
              ret_decay_f, ret_decay_b, w_out, w_ada, b_ada, ln1_g, ln1_b, ln2_g, ln2_b,
              router_w, router_b, exp_w_gate, exp_w_up, exp_w_down, sh_w_gate, sh_w_up, sh_w_down):
    y_p = x_prompt
    y_s = x_sample
    cond_ctx = c_ctx[None, :]
    zero_state = jnp.zeros((x_prompt.shape[0], RET_HEADS, RET_DK, RET_DK), jnp.float32)
    new_f = []
    new_b = []
    for l in range(DEPTH):
        lp = {
            'w_in': w_in[l], 'hy_conv_w': hy_conv_w[l], 'hy_conv_b': hy_conv_b[l],
            'hy_ffn_w1': hy_ffn_w1[l], 'hy_ffn_b1': hy_ffn_b1[l], 'hy_ffn_w2': hy_ffn_w2[l],
            'hy_ffn_b2': hy_ffn_b2[l], 'hy_ffn_w3': hy_ffn_w3[l], 'hy_sin_freq': hy_sin_freq[l],
            'hy_bias': hy_bias[l], 'ret_decay_f': ret_decay_f[l], 'ret_decay_b': ret_decay_b[l],
            'w_out': w_out[l], 'w_ada': w_ada[l], 'b_ada': b_ada[l],
            'ln1_g': ln1_g[l], 'ln1_b': ln1_b[l], 'ln2_g': ln2_g[l], 'ln2_b': ln2_b[l],
            'router_w': router_w[l], 'router_b': router_b[l], 'exp_w_gate': exp_w_gate[l],
            'exp_w_up': exp_w_up[l], 'exp_w_down': exp_w_down[l], 'sh_w_gate': sh_w_gate[l],
            'sh_w_up': sh_w_up[l], 'sh_w_down': sh_w_down[l],
        }
        y_p, s_f, s_b = _layer(y_p, cond_ctx, zero_state, zero_state, False, lp)
        new_f.append(s_f.astype(x_prompt.dtype))
        new_b.append(s_b.astype(x_prompt.dtype))
        y_s, _, _ = _layer(y_s, c, state_ret_fwd[:, l], state_ret_bwd[:, l], True, lp)
    new_state_ret_fwd = jnp.stack(new_f, axis=1)
    new_state_ret_bwd = jnp.stack(new_b, axis=1)
    return (y_p, y_s, new_state_ret_fwd, new_state_ret_bwd)
```

```python
import functools
import math

import numpy as np
import jax
import jax.numpy as jnp
from jax import lax
from jax.experimental import pallas as pl
from jax.experimental.pallas import tpu as pltpu

F32 = jnp.float32
BF16 = jnp.bfloat16

RET_HEADS = 8
TOP_K = 8
GRID_W = 64
HY_ORDER = 2
HY_BANDS = 16
HY_TARGET = 1e-2
HY_FAST = 0.3
HY_SLOW = 1.5
ROUTED_SCALE = 2.5
ROPE_BASE = 10000.0
LN_EPS = 1e-5
GN_EPS = 1e-6

V7X_LANES = 128
V7X_VMEM_LIMIT_BYTES = 56 * 1024 * 1024

DFT_BLOCK = 256
RET_CHUNK = 128
MOE_ROWS = 512
TOK_SUB = 16


def _cparams(sem, vmem=V7X_VMEM_LIMIT_BYTES):
    return pltpu.CompilerParams(dimension_semantics=sem, vmem_limit_bytes=vmem)


def _dot(a, b):
    return jnp.dot(a, b, preferred_element_type=F32)


def _split_bf16(a):
    hi = a.astype(BF16)
    lo = (a - hi.astype(F32)).astype(BF16)
    return hi, lo


def _dot3(a, b):
    ah, al = _split_bf16(a)
    bh, bl = _split_bf16(b)
    return _dot(ah, bh) + _dot(ah, bl) + _dot(al, bh)


def _silu(x):
    return x * jax.nn.sigmoid(x)


def _whole(shape):
    n = len(shape)
    return pl.BlockSpec(shape, lambda *_: (0,) * n)


def _ada_kernel(c_ref, w_ref, b_ref, o_ref):
    o_ref[...] = _dot3(_silu(c_ref[...]), w_ref[...]) + b_ref[...]


def _ada(cond, w_ada, b_ada, tn=1024):
    rows, d = cond.shape
    n = w_ada.shape[1]
    return pl.pallas_call(
        _ada_kernel,
        grid=(n // tn,),
        in_specs=[pl.BlockSpec((rows, d), lambda j: (0, 0)),
                  pl.BlockSpec((d, tn), lambda j: (0, j)),
                  pl.BlockSpec((1, tn), lambda j: (0, j))],
        out_specs=pl.BlockSpec((rows, tn), lambda j: (0, j)),
        out_shape=jax.ShapeDtypeStruct((rows, n), F32),
        compiler_params=_cparams(("parallel",)),
    )(cond, w_ada, b_ada.reshape(1, n))


def _proj_kernel(x_ref, sc_ref, sh_ref, w_ref, o_ref, h_scr):
    @pl.when(pl.program_id(1) == 0)
    def _():
        h_scr[...] = (x_ref[...] * (1.0 + sc_ref[0]) + sh_ref[0]).astype(BF16)

    o_ref[...] = _dot(h_scr[...], w_ref[...]).astype(o_ref.dtype)


def _proj(x, mods3, row_of_tile, w_in_bf16, tm, tn=1024):
    t, d = x.shape
    n = w_in_bf16.shape[1]
    return pl.pallas_call(
        _proj_kernel,
        grid=(t // tm, n // tn),
        in_specs=[pl.BlockSpec((tm, d), lambda i, j: (i, 0)),
                  pl.BlockSpec((1, 1, d), lambda i, j: (row_of_tile(i), 0, 1)),
                  pl.BlockSpec((1, 1, d), lambda i, j: (row_of_tile(i), 0, 0)),
                  pl.BlockSpec((d, tn), lambda i, j: (0, j))],
        out_specs=pl.BlockSpec((tm, tn), lambda i, j: (i, j)),
        out_shape=jax.ShapeDtypeStruct((t, n), BF16),
        scratch_shapes=[pltpu.VMEM((tm, d), BF16)],
        compiler_params=_cparams(("parallel", "arbitrary")),
    )(x, mods3, mods3, w_in_bf16)


@functools.lru_cache(maxsize=None)
def _hy_mats(L):
    nn2 = L // DFT_BLOCK
    n_full = 2 * L
    n2_full = 2 * nn2
    g_rows = DFT_BLOCK // nn2
    k2 = np.arange(nn2)
    n2 = np.arange(nn2)
    s1 = np.zeros((nn2, DFT_BLOCK, DFT_BLOCK), np.complex128)
    for g in range(nn2):
        for s in range(g_rows):
            n1 = g * g_rows + s
            blk = (np.exp(-2j * np.pi * np.outer(k2 + 0.5, n2) / n2_full)
                   * np.exp(-2j * np.pi * (k2 + 0.5) * n1 / n_full)[:, None])
            s1[g][np.ix_(k2 * g_rows + s, n2 * g_rows + s)] = blk
    m1f = np.concatenate([s1.real, s1.imag], axis=1)
    a = np.conj(s1).transpose(0, 2, 1)
    m1i = (2.0 / n_full) * np.concatenate([a.real, -a.imag], axis=2)
    k1 = np.arange(DFT_BLOCK)
    f = np.exp(-2j * np.pi * np.outer(k1, k1) / DFT_BLOCK)
    f2f = np.block([[f.real, -f.imag], [f.imag, f.real]])
    f2i = np.block([[f.real, f.imag], [-f.imag, f.real]])
    return tuple(np.asarray(m, np.float32) for m in (m1f, m1i, f2f, f2i))


def _hy_fwd_stage1(u_scr, m1f_ref, v_scr, nn2):
    g_rows = DFT_BLOCK // nn2

    def body(g, carry):
        off = pl.multiple_of(g * g_rows, g_rows)
        parts = [u_scr[pl.ds(n2 * DFT_BLOCK + off, g_rows), :] for n2 in range(nn2)]
        inp = jnp.concatenate(parts, axis=0).astype(BF16)
        out = _dot(m1f_ref[g], inp)
        for ri in range(2):
            for k2 in range(nn2):
                r0 = (ri * nn2 + k2) * g_rows
                v_scr[k2, pl.ds(ri * DFT_BLOCK + off, g_rows), :] = out[r0:r0 + g_rows].astype(BF16)
        return carry

    lax.fori_loop(0, nn2, body, 0)


def _hy_inv_stage1(w_scr, m1i_ref, y_scr, nn2):
    g_rows = DFT_BLOCK // nn2

    def body(g, carry):
        off = pl.multiple_of(g * g_rows, g_rows)
        parts = [w_scr[k2, pl.ds(ri * DFT_BLOCK + off, g_rows), :]
                 for ri in range(2) for k2 in range(nn2)]
        inp = jnp.concatenate(parts, axis=0)
        out = _dot(m1i_ref[g], inp)
        for n2 in range(nn2):
            y_scr[pl.ds(n2 * DFT_BLOCK + off, g_rows), :] = out[n2 * g_rows:(n2 + 1) * g_rows]
        return carry

    lax.fori_loop(0, nn2, body, 0)


def _short_conv(u, w, b):
    L = u.shape[0]
    row = lax.broadcasted_iota(jnp.int32, u.shape, 0)
    prev = jnp.where(row == 0, 0.0, pltpu.roll(u, 1, 0))
    nxt = jnp.where(row == L - 1, 0.0, pltpu.roll(u, L - 1, 0))
    return prev * w[0:1] + u * w[1:2] + nxt * w[2:3] + b


def _hy_filter_kernel(z_ref, w1_ref, b1_ref, w2_ref, b2_ref, w3_ref, fr_ref, dl_ref, o_ref, *, n_half_tiles, c):
    i = pl.program_id(0)
    z = z_ref[...]
    fr = fr_ref[...]
    h = jnp.sin(fr * (_dot3(z, w1_ref[...]) + b1_ref[...]))
    h = jnp.sin(fr * (_dot3(h, w2_ref[...]) + b2_ref[...]))
    h = _dot3(h, w3_ref[...])
    tn = z[:, 0:1]
    decay = jnp.exp(-tn * jnp.abs(dl_ref[...]))
    valid = z[:, 2 * HY_BANDS + 1:2 * HY_BANDS + 2]
    second = i >= n_half_tiles
    for o in range(HY_ORDER):
        fwd = h[:, (2 * o) * c:(2 * o + 1) * c]
        bwd = h[:, (2 * o + 1) * c:(2 * o + 2) * c]
        val = jnp.where(second, -bwd, fwd) * decay * valid
        o_ref[:, o * c:(o + 1) * c] = val


def _hy_filters(L, w1, b1, w2, b2, w3, freq, rt=256):
    c = w3.shape[1] // (2 * HY_ORDER)
    fh = w1.shape[1]
    r = np.arange(2 * L)
    t = np.where(r < L, r, 2 * L - r).astype(np.float64)
    f = np.linspace(1e-4, HY_BANDS - 1, HY_BANDS)
    ang = (2.0 * math.pi / L) * t[:, None] * f[None, :]
    z = np.zeros((2 * L, V7X_LANES), np.float32)
    z[:, 0] = t / (L - 1)
    z[:, 1:1 + HY_BANDS] = np.cos(ang)
    z[:, 1 + HY_BANDS:1 + 2 * HY_BANDS] = -np.sin(ang)
    z[:, 1 + 2 * HY_BANDS] = (r != L)
    w1p = jnp.zeros((V7X_LANES, fh), F32).at[:w1.shape[0]].set(w1)
    deltas = np.linspace(math.log(HY_TARGET) / HY_SLOW, math.log(HY_TARGET) / HY_FAST, c).astype(np.float32)
    kern = functools.partial(_hy_filter_kernel, n_half_tiles=L // rt, c=c)
    return pl.pallas_call(
        kern,
        grid=(2 * L // rt,),
        in_specs=[pl.BlockSpec((rt, V7X_LANES), lambda i: (i, 0)),
                  _whole((V7X_LANES, fh)), _whole((1, fh)), _whole((fh, fh)), _whole((1, fh)),
                  _whole(w3.shape), _whole((1, fh)), _whole((1, c))],
        out_specs=pl.BlockSpec((rt, HY_ORDER * c), lambda i: (i, 0)),
        out_shape=jax.ShapeDtypeStruct((2 * L, HY_ORDER * c), F32),
        compiler_params=_cparams(("parallel",)),
    )(jnp.asarray(z), w1p, b1.reshape(1, fh), w2, b2.reshape(1, fh), w3, freq.reshape(1, fh),
      jnp.asarray(deltas).reshape(1, c))


def _hy_spec_kernel(f_ref, m1f_ref, f2f_ref, o_ref, u_scr, v_scr, acc_scr, *, nn2):
    half = pl.program_id(1)
    u_scr[...] = f_ref[...]
    _hy_fwd_stage1(u_scr, m1f_ref, v_scr, nn2)

    def body(k2, carry):
        z = _dot(f2f_ref[...], v_scr[k2])

        @pl.when(half == 0)
        def _():
            acc_scr[k2] = z

        @pl.when(half == 1)
        def _():
            sign = 1.0 - 2.0 * (k2 % 2)
            a = acc_scr[k2]
            o_ref[0, k2] = (a[:DFT_BLOCK] + sign * z[DFT_BLOCK:]).astype(o_ref.dtype)
            o_ref[1, k2] = (a[DFT_BLOCK:] - sign * z[:DFT_BLOCK]).astype(o_ref.dtype)

        return carry

    lax.fori_loop(0, nn2, body, 0)


def _hy_spectrum(filt, L, ct=256):
    nn2 = L // DFT_BLOCK
    cols = filt.shape[1]
    m1f, _, f2f, _ = _hy_mats(L)
    kern = functools.partial(_hy_spec_kernel, nn2=nn2)
    return pl.pallas_call(
        kern,
        grid=(cols // ct, 2),
        in_specs=[pl.BlockSpec((L, ct), lambda j, h: (h, j)),
                  pl.BlockSpec(memory_space=pltpu.VMEM),
                  pl.BlockSpec(memory_space=pltpu.VMEM)],
        out_specs=pl.BlockSpec((2, nn2, DFT_BLOCK, ct), lambda j, h: (0, 0, 0, j)),
        out_shape=jax.ShapeDtypeStruct((2, nn2, DFT_BLOCK, cols), BF16),
        scratch_shapes=[pltpu.VMEM((L, ct), F32),
                        pltpu.VMEM((nn2, 2 * DFT_BLOCK, ct), BF16),
                        pltpu.VMEM((nn2, 2 * DFT_BLOCK, ct), F32)],
        compiler_params=_cparams(("parallel", "arbitrary")),
    )(filt, jnp.asarray(m1f, BF16), jnp.asarray(f2f, BF16))


def _hy_conv_kernel(u_ref, g_ref, cwu_ref, cbu_ref, cwg_ref, cbg_ref, bias_ref, ks_ref,
                    m1f_ref, m1i_ref, f2f_ref, f2i_ref, o_ref, u_scr, v_scr, w_scr, y_scr, *, nn2, conv_u):
    u = u_ref[...].astype(F32)
    if conv_u:
        u = _short_conv(u, cwu_ref[...], cbu_ref[...])
    u_scr[...] = u
    _hy_fwd_stage1(u_scr, m1f_ref, v_scr, nn2)

    def body(k2, carry):
        z = _dot(f2f_ref[...], v_scr[k2])
        zr, zi = z[:DFT_BLOCK], z[DFT_BLOCK:]
        kr = ks_ref[0, k2].astype(F32)
        ki = ks_ref[1, k2].astype(F32)
        y = jnp.concatenate([zr * kr - zi * ki, zr * ki + zi * kr], axis=0).astype(BF16)
        w_scr[k2] = _dot(f2i_ref[...], y).astype(BF16)
        return carry

    lax.fori_loop(0, nn2, body, 0)
    _hy_inv_stage1(w_scr, m1i_ref, y_scr, nn2)
    gate = _short_conv(g_ref[...].astype(F32), cwg_ref[...], cbg_ref[...])
    o_ref[...] = (gate * (y_scr[...] + bias_ref[...] * u_scr[...])).astype(o_ref.dtype)


def _hy_conv(u_arr, u_col0, conv_u, proj, g_col0, row_blk0, n_seq, L, conv_w, conv_b, bias_o, spec, spec_col0, c, ct=256):
    nn2 = L // DFT_BLOCK
    m1f, m1i, f2f, f2i = (jnp.asarray(m, BF16) for m in _hy_mats(L))
    nct = c // ct
    ub, gb, sb = u_col0 // ct, g_col0 // ct, spec_col0 // ct
    u_row0 = row_blk0 if conv_u else 0
    kern = functools.partial(_hy_conv_kernel, nn2=nn2, conv_u=conv_u)
    vm = pl.BlockSpec(memory_space=pltpu.VMEM)
    return pl.pallas_call(
        kern,
        grid=(nct, n_seq),
        in_specs=[pl.BlockSpec((L, ct), lambda j, b: (u_row0 + b, ub + j)),
                  pl.BlockSpec((L, ct), lambda j, b: (row_blk0 + b, gb + j)),
                  pl.BlockSpec((3, ct), lambda j, b: (0, ub + j)),
                  pl.BlockSpec((1, ct), lambda j, b: (0, ub + j)),
                  pl.BlockSpec((3, ct), lambda j, b: (0, gb + j)),
                  pl.BlockSpec((1, ct), lambda j, b: (0, gb + j)),
                  pl.BlockSpec((1, ct), lambda j, b: (0, j)),
                  pl.BlockSpec((2, nn2, DFT_BLOCK, ct), lambda j, b: (0, 0, 0, sb + j)),
                  vm, vm, vm, vm],
        out_specs=pl.BlockSpec((L, ct), lambda j, b: (b, j)),
        out_shape=jax.ShapeDtypeStruct((n_seq * L, c), BF16),
        scratch_shapes=[pltpu.VMEM((L, ct), F32),
                        pltpu.VMEM((nn2, 2 * DFT_BLOCK, ct), BF16),
                        pltpu.VMEM((nn2, 2 * DFT_BLOCK, ct), BF16),
                        pltpu.VMEM((L, ct), F32)],
        compiler_params=_cparams(("parallel", "parallel")),
    )(u_arr, proj, conv_w, conv_b, conv_w, conv_b, bias_o, spec, m1f, m1i, f2f, f2i)


def _hyena(proj, row_blk0, n_seq, L, lp, c):
    filt = _hy_filters(L, lp['hy_ffn_w1'], lp['hy_ffn_b1'], lp['hy_ffn_w2'], lp['hy_ffn_b2'],
                       lp['hy_ffn_w3'], lp['hy_sin_freq'])
    spec = _hy_spectrum(filt, L)
    cw = lp['hy_conv_w']
    cb = lp['hy_conv_b'].reshape(1, -1)
    bias = lp['hy_bias']
    z = _hy_conv(proj, 0, True, proj, c, row_blk0, n_seq, L, cw, cb, bias[0:1], spec, 0, c)
    z = _hy_conv(z, 0, False, proj, 2 * c, row_blk0, n_seq, L, cw, cb, bias[1:2], spec, c, c)
    return z


def _rope_tables(L, dk):
    rows = L // GRID_W
    row = np.repeat(np.arange(rows), GRID_W).astype(np.float64)
    col = np.tile(np.arange(GRID_W), rows).astype(np.float64)
    quarter = dk // 4
    inv = ROPE_BASE ** (-np.arange(quarter, dtype=np.float64) / quarter)
    ar = row[:, None] * inv[None, :]
    ac = col[:, None] * inv[None, :]
    cos = np.concatenate([np.cos(ar), np.cos(ar), np.cos(ac), np.cos(ac)], axis=1)
    sin = np.concatenate([-np.sin(ar), np.sin(ar), -np.sin(ac), np.sin(ac)], axis=1)
    return jnp.asarray(cos, F32), jnp.asarray(sin, F32)


def _rope(x, cos, sin_signed):
    dk = x.shape[1]
    q = dk // 4
    lane = lax.broadcasted_iota(jnp.int32, x.shape, 1)
    first = (lane % (2 * q)) < q
    swapped = jnp.where(first, pltpu.roll(x, dk - q, 1), pltpu.roll(x, q, 1))
    return x * cos + swapped * sin_signed


def _ret_kernel(lg_ref, q_ref, k_ref, v_ref, g_ref, cos_ref, sin_ref, s0f_ref, s0b_ref,
                y_ref, sf_ref, sb_ref, q_scr, k_scr, v_scr, o_scr, *, latent, chunk):
    h = pl.program_id(1)
    L, dk = q_ref.shape
    n = L // chunk
    q = q_ref[...].astype(F32)
    k = k_ref[...].astype(F32) * (dk ** -0.5)
    if latent:
        q = _rope(q, cos_ref[...], sin_ref[...])
        k = _rope(k, cos_ref[...], sin_ref[...])
    q_scr[...] = q
    k_scr[...] = k
    v_scr[...] = v_ref[...]

    pos_r = lax.broadcasted_iota(jnp.int32, (chunk, chunk), 0).astype(F32)
    pos_c = lax.broadcasted_iota(jnp.int32, (chunk, chunk), 1).astype(F32)
    pos = lax.broadcasted_iota(jnp.int32, (chunk, 1), 0).astype(F32)

    def direction(lg, s0, reverse):
        if reverse:
            diff = pos_c - pos_r
            q_dec = jnp.exp(lg * (chunk - pos))
            k_dec = jnp.exp(lg * pos)
        else:
            diff = pos_r - pos_c
            q_dec = jnp.exp(lg * (pos + 1.0))
            k_dec = jnp.exp(lg * (chunk - 1.0 - pos))
        intra = jnp.where(diff >= 0, jnp.exp(lg * jnp.maximum(diff, 0.0)), 0.0)
        c_dec = jnp.exp(lg * chunk)

        def body(i, s):
            ci = (n - 1 - i) if reverse else i
            r0 = pl.multiple_of(ci * chunk, chunk)
            qi = q_scr[pl.ds(r0, chunk), :]
            ki = k_scr[pl.ds(r0, chunk), :]
            vi = v_scr[pl.ds(r0, chunk), :]
            att = lax.dot_general(qi.astype(BF16), ki.astype(BF16), (((1,), (1,)), ((), ())),
                                  preferred_element_type=F32) * intra
            o = _dot(att.astype(BF16), vi) + _dot((qi * q_dec).astype(BF16), s.astype(BF16))
            if reverse:
                o_scr[pl.ds(r0, chunk), :] += o
            else:
                o_scr[pl.ds(r0, chunk), :] = o
            kd_t = (ki * k_dec).T.astype(BF16)
            return c_dec * s + _dot(kd_t, vi)

        return lax.fori_loop(0, n, body, s0)

    s_f = direction(lg_ref[0, h], s0f_ref[0, 0], False)
    s_b = direction(lg_ref[1, h], s0b_ref[0, 0], True)
    sf_ref[0, 0] = s_f
    sb_ref[0, 0] = s_b

    o = o_scr[...]
    mu = jnp.mean(o, axis=-1, keepdims=True)
    var = jnp.mean(jnp.square(o - mu), axis=-1, keepdims=True)
    o = (o - mu) * lax.rsqrt(var + GN_EPS)
    y_ref[...] = (o * _silu(g_ref[...].astype(F32))).astype(y_ref.dtype)


def _retention(proj, row_blk0, n_seq, L, col0, lg, s0_f, s0_b, latent):
    dk = s0_f.shape[-1]
    hb = col0 // dk
    cos, sin = _rope_tables(L, dk) if latent else (jnp.zeros((8, dk), F32), jnp.zeros((8, dk), F32))
    tbl = pl.BlockSpec(cos.shape, lambda b, h: (0, 0))
    seq = lambda part: pl.BlockSpec((L, dk), lambda b, h: (row_blk0 + b, hb + part * RET_HEADS + h))
    st = pl.BlockSpec((1, 1, dk, dk), lambda b, h: (b, h, 0, 0))
    kern = functools.partial(_ret_kernel, latent=latent, chunk=RET_CHUNK)
    return pl.pallas_call(
        kern,
        grid=(n_seq, RET_HEADS),
        in_specs=[pl.BlockSpec(memory_space=pltpu.SMEM),
                  seq(0), seq(1), seq(2), seq(3), tbl, tbl, st, st],
        out_specs=[pl.BlockSpec((L, dk), lambda b, h: (b, h)), st, st],
        out_shape=[jax.ShapeDtypeStruct((n_seq * L, RET_HEADS * dk), BF16),
                   jax.ShapeDtypeStruct((n_seq, RET_HEADS, dk, dk), F32),
                   jax.ShapeDtypeStruct((n_seq, RET_HEADS, dk, dk), F32)],
        scratch_shapes=[pltpu.VMEM((L, dk), F32), pltpu.VMEM((L, dk), F32),
                        pltpu.VMEM((L, dk), BF16), pltpu.VMEM((L, dk), F32)],
        compiler_params=_cparams(("parallel", "parallel")),
    )(lg, proj, proj, proj, proj, cos, sin, s0_f, s0_b)


def _layernorm(r, g, b):
    mu = jnp.mean(r, axis=-1, keepdims=True)
    var = jnp.mean(jnp.square(r - mu), axis=-1, keepdims=True)
    return (r - mu) * lax.rsqrt(var + LN_EPS) * g + b


def _out_kernel(yh_ref, yr_ref, x_ref, g1_ref, sc2_ref, sh2_ref, g2_ref, lng_ref, lnb_ref,
                wo_ref, wg_ref, wu_ref, wd_ref, rwh_ref, rwl_ref,
                base_ref, h2_ref, lo_ref, *, alpha):
    c = yh_ref.shape[1]
    mix = _dot(yh_ref[...], wo_ref[:c]) + _dot(yr_ref[...], wo_ref[c:])
    x1 = _layernorm(alpha * x_ref[...] + g1_ref[0] * mix, lng_ref[...], lnb_ref[...])
    h2 = x1 * (1.0 + sc2_ref[0]) + sh2_ref[0]
    _to_token_tiles(h2_ref, h2)
    hh, hl = _split_bf16(h2)
    lo_ref[...] = _dot(hh, rwh_ref[...]) + _dot(hh, rwl_ref[...]) + _dot(hl, rwh_ref[...])
    a = (_silu(_dot(hh, wg_ref[...])) * _dot(hh, wu_ref[...])).astype(BF16)
    base_ref[...] = alpha * x1 + g2_ref[0] * _dot(a, wd_ref[...])


def _out_stage(y_hy, y_ret, x, mods3, row_of_tile, lp, alpha, tm):
    t, d = x.shape
    c = y_hy.shape[1]
    e = lp['router_w'].shape[1]
    ep = max(e, V7X_LANES)
    rw = jnp.zeros((d, ep), F32).at[:, :e].set(lp['router_w'])
    rwh = rw.astype(BF16)
    rwl = (rw - rwh.astype(F32)).astype(BF16)
    fs = lp['sh_w_gate'].shape[1]
    mod = lambda k: pl.BlockSpec((1, 1, d), lambda i: (row_of_tile(i), 0, k))
    vm = pl.BlockSpec(memory_space=pltpu.VMEM)
    tile = lambda w: pl.BlockSpec((tm, w), lambda i: (i, 0))
    kern = functools.partial(_out_kernel, alpha=alpha)
    return pl.pallas_call(
        kern,
        grid=(t // tm,),
        in_specs=[tile(c), tile(c), tile(d), mod(2), mod(4), mod(3), mod(5),
                  _whole((1, d)), _whole((1, d)), vm, vm, vm, vm, vm, vm],
        out_specs=[tile(d), pl.BlockSpec((tm * TOK_SUB, d // TOK_SUB), lambda i: (i, 0)), tile(ep)],
        out_shape=[jax.ShapeDtypeStruct((t, d), F32), jax.ShapeDtypeStruct((t * TOK_SUB, d // TOK_SUB), F32),
                   jax.ShapeDtypeStruct((t, ep), F32)],
        compiler_params=_cparams(("parallel",)),
    )(y_hy, y_ret, x, mods3, mods3, mods3, mods3, lp['ln1_g'].reshape(1, d), lp['ln1_b'].reshape(1, d),
      lp['w_out'].astype(BF16), lp['sh_w_gate'].astype(BF16), lp['sh_w_up'].astype(BF16),
      lp['sh_w_down'].astype(BF16), rwh, rwl)


def _route_kernel(lo_ref, rb_ref, tri_ref, idx_ref, gate_ref, rank_ref, cnt_ref, carry_scr, *, n_exp):
    i = pl.program_id(0)

    @pl.when(i == 0)
    def _():
        carry_scr[...] = jnp.zeros_like(carry_scr)

    logits = lo_ref[...]
    tm, ep = logits.shape
    lane = lax.broadcasted_iota(jnp.int32, (tm, ep), 1).astype(F32)
    real = lane < n_exp
    scores = jax.nn.sigmoid(logits)
    work = jnp.where(real, scores + rb_ref[...], -jnp.inf)
    sel = jnp.zeros((tm, ep), jnp.bool_)
    idx_cols = []
    s_cols = []
    for _ in range(TOP_K):
        m = jnp.max(work, axis=-1, keepdims=True)
        first = jnp.min(jnp.where(work == m, lane, float(ep)), axis=-1, keepdims=True)
        hit = lane == first
        idx_cols.append(first)
        s_cols.append(jnp.sum(jnp.where(hit, scores, 0.0), axis=-1, keepdims=True))
        sel = jnp.logical_or(sel, hit)
        work = jnp.where(hit, -jnp.inf, work)
    denom = functools.reduce(lambda a, b: a + b, s_cols)
    onehot = sel.astype(BF16)
    before = _dot(tri_ref[...], onehot) + carry_scr[...]
    idx_out = jnp.zeros((tm, ep), F32)
    gate_out = jnp.zeros((tm, ep), F32)
    rank_out = jnp.zeros((tm, ep), jnp.int32)
    for j in range(TOP_K):
        hit = lane == idx_cols[j]
        rank_j = jnp.sum(jnp.where(hit, before, 0.0), axis=-1, keepdims=True).astype(jnp.int32)
        idx_out = jnp.where(lane == j, idx_cols[j], idx_out)
        gate_out = jnp.where(lane == j, ROUTED_SCALE * s_cols[j] / denom, gate_out)
        rank_out = jnp.where(lane == j, rank_j, rank_out)
    idx_ref[...] = idx_out.astype(jnp.int32)
    gate_ref[...] = gate_out
    rank_ref[...] = rank_out
    carry_scr[...] += jnp.sum(sel.astype(F32), axis=0, keepdims=True)
    cnt_ref[...] = carry_scr[...]


def _route(logits, router_b, n_exp, tm=256):
    t, ep = logits.shape
    rb = jnp.zeros((1, ep), F32).at[0, :n_exp].set(router_b)
    tri = jnp.asarray(np.tril(np.ones((tm, tm), np.float32), -1), BF16)
    tile = pl.BlockSpec((tm, ep), lambda i: (i, 0))
    kern = functools.partial(_route_kernel, n_exp=n_exp)
    return pl.pallas_call(
        kern,
        grid=(t // tm,),
        in_specs=[tile, _whole((1, ep)), _whole((tm, tm))],
        out_specs=[tile, tile, tile, _whole((1, ep))],
        out_shape=[jax.ShapeDtypeStruct((t, ep), jnp.int32), jax.ShapeDtypeStruct((t, ep), F32),
                   jax.ShapeDtypeStruct((t, ep), jnp.int32), jax.ShapeDtypeStruct((1, ep), F32)],
        scratch_shapes=[pltpu.VMEM((1, ep), F32)],
        compiler_params=_cparams(("arbitrary",)),
    )(logits, rb, tri)


def _to_token_tiles(ref, val):
    rows, d = val.shape
    w = d // TOK_SUB
    for s in range(TOK_SUB):
        ref[pl.ds(s, rows, stride=TOK_SUB), :] = val[:, s * w:(s + 1) * w]


def _from_token_tiles(ref, row0, rows):
    return jnp.concatenate([ref[pl.ds(row0 * TOK_SUB + s, rows, stride=TOK_SUB), :] for s in range(TOK_SUB)],
                           axis=1)


def _row_copy(src_ref, src_row, dst_ref, dst_row, sem):
    src = src_ref.at[pl.ds(pl.multiple_of(src_row * TOK_SUB, TOK_SUB), TOK_SUB), :]
    dst = dst_ref.at[pl.ds(pl.multiple_of(dst_row * TOK_SUB, TOK_SUB), TOK_SUB), :]
    return pltpu.make_async_copy(src, dst, sem)


def _dispatch_kernel(dest_ref, zpos_ref, h_ref, xs_ref, zero_scr, sem, *, n_exp, tm):
    i = pl.program_id(0)

    @pl.when(i == 0)
    def _():
        zero_scr[...] = jnp.zeros_like(zero_scr)
        rows = zero_scr.shape[0]

        def zcopy(e):
            start = pl.multiple_of(zpos_ref[e] * TOK_SUB, TOK_SUB)
            return pltpu.make_async_copy(zero_scr, xs_ref.at[pl.ds(start, rows), :], sem)

        def start(e, c):
            zcopy(e).start()
            return c

        def wait(e, c):
            zcopy(e).wait()
            return c

        lax.fori_loop(0, n_exp, start, 0)
        lax.fori_loop(0, n_exp, wait, 0)

    def start(t, c):
        for j in range(TOP_K):
            _row_copy(h_ref, t, xs_ref, dest_ref[t * TOP_K + j], sem).start()
        return c

    def wait(t, c):
        for j in range(TOP_K):
            _row_copy(h_ref, t, xs_ref, dest_ref[t * TOP_K + j], sem).wait()
        return c

    lax.fori_loop(0, tm, start, 0)
    lax.fori_loop(0, tm, wait, 0)


def _dispatch(h2t, dest_flat, zpos, p_rows, n_exp, tm=256):
    w = h2t.shape[1]
    t = h2t.shape[0] // TOK_SUB
    kern = functools.partial(_dispatch_kernel, n_exp=n_exp, tm=tm)
    return pl.pallas_call(
        kern,
        grid=(t // tm,),
        in_specs=[pl.BlockSpec((tm * TOP_K,), lambda i: (i,), memory_space=pltpu.SMEM),
                  pl.BlockSpec(memory_space=pltpu.SMEM),
                  pl.BlockSpec((tm * TOK_SUB, w), lambda i: (i, 0))],
        out_specs=pl.BlockSpec(memory_space=pl.ANY),
        out_shape=jax.ShapeDtypeStruct((p_rows * TOK_SUB, w), F32),
        scratch_shapes=[pltpu.VMEM((MOE_ROWS * TOK_SUB, w), F32), pltpu.SemaphoreType.DMA(())],
        compiler_params=_cparams(("arbitrary",)),
    )(dest_flat, zpos, h2t)


def _gmm_kernel(be_ref, nu_ref, x_ref, wg_ref, wu_ref, wd_ref, o_ref):
    b = pl.program_id(0)
    rows = x_ref.shape[0] // TOK_SUB

    @pl.when(b < nu_ref[0])
    def _():
        x = _from_token_tiles(x_ref, 0, rows).astype(BF16)
        a = (_silu(_dot(x, wg_ref[0])) * _dot(x, wu_ref[0])).astype(BF16)
        _to_token_tiles(o_ref, _dot(a, wd_ref[0]))

    @pl.when(b >= nu_ref[0])
    def _():
        o_ref[...] = jnp.zeros_like(o_ref)


def _gmm(xs, blk_e, n_used, wg, wu, wd, nb):
    w = xs.shape[1]
    d, f = wg.shape[1], wg.shape[2]
    r8 = MOE_ROWS * TOK_SUB
    last = lambda b, nu: jnp.minimum(b, nu[0] - 1)
    grid_spec = pltpu.PrefetchScalarGridSpec(
        num_scalar_prefetch=2,
        grid=(nb,),
        in_specs=[pl.BlockSpec((r8, w), lambda b, be, nu: (last(b, nu), 0)),
                  pl.BlockSpec((1, d, f), lambda b, be, nu: (be[last(b, nu)], 0, 0)),
                  pl.BlockSpec((1, d, f), lambda b, be, nu: (be[last(b, nu)], 0, 0)),
                  pl.BlockSpec((1, f, d), lambda b, be, nu: (be[last(b, nu)], 0, 0))],
        out_specs=pl.BlockSpec((r8, w), lambda b, be, nu: (b, 0)),
    )
    return pl.pallas_call(
        _gmm_kernel,
        grid_spec=grid_spec,
        out_shape=jax.ShapeDtypeStruct((nb * r8, w), F32),
        compiler_params=_cparams(("arbitrary",)),
    )(blk_e, n_used, xs, wg, wu, wd)


def _combine_kernel(dest_ref, ys_ref, gate_ref, base_ref, g2_ref, lng_ref, lnb_ref, o_ref, rows_scr, sem, *, tm):
    def start(t, c):
        for j in range(TOP_K):
            _row_copy(ys_ref, dest_ref[t * TOP_K + j], rows_scr, j * tm + t, sem).start()
        return c

    def wait(t, c):
        for j in range(TOP_K):
            _row_copy(ys_ref, dest_ref[t * TOP_K + j], rows_scr, j * tm + t, sem).wait()
        return c

    lax.fori_loop(0, tm, start, 0)
    lax.fori_loop(0, tm, wait, 0)
    gates = gate_ref[...]
    routed = _from_token_tiles(rows_scr, 0, tm) * gates[:, 0:1]
    for j in range(1, TOP_K):
        routed = routed + _from_token_tiles(rows_scr, j * tm, tm) * gates[:, j:j + 1]
    o_ref[...] = _layernorm(base_ref[...] + g2_ref[0] * routed, lng_ref[...], lnb_ref[...])


def _combine(ys, dest_flat, gates, base, mods3, row_of_tile, ln_g, ln_b, tm=64):
    t, d = base.shape
    ep = gates.shape[1]
    kern = functools.partial(_combine_kernel, tm=tm)
    return pl.pallas_call(
        kern,
        grid=(t // tm,),
        in_specs=[pl.BlockSpec((tm * TOP_K,), lambda i: (i,), memory_space=pltpu.SMEM),
                  pl.BlockSpec(memory_space=pl.ANY),
                  pl.BlockSpec((tm, ep), lambda i: (i, 0)),
                  pl.BlockSpec((tm, d), lambda i: (i, 0)),
                  pl.BlockSpec((1, 1, d), lambda i: (row_of_tile(i, tm), 0, 5)),
                  _whole((1, d)), _whole((1, d))],
        out_specs=pl.BlockSpec((tm, d), lambda i: (i, 0)),
        out_shape=jax.ShapeDtypeStruct((t, d), F32),
        scratch_shapes=[pltpu.VMEM((TOP_K * tm * TOK_SUB, d // TOK_SUB), F32), pltpu.SemaphoreType.DMA(())],
        compiler_params=_cparams(("arbitrary",)),
    )(dest_flat, ys, gates, base, mods3, ln_g.reshape(1, d), ln_b.reshape(1, d))


def _moe(h2, logits, base, mods3, row_of_tile, lp):
    t = h2.shape[0] // TOK_SUB
    n_exp = lp['router_w'].shape[1]
    r = MOE_ROWS
    idx_p, gate_p, rank_p, cnt_p = _route(logits, lp['router_b'], n_exp)
    counts = cnt_p[0, :n_exp].astype(jnp.int32)
    padded = (counts + r - 1) // r * r
    pend = jnp.cumsum(padded)
    pstart = pend - padded
    dest = (pstart[idx_p[:, :TOP_K]] + rank_p[:, :TOP_K]).reshape(t * TOP_K)
    nb = (t * TOP_K) // r + n_exp
    blk_e = jnp.clip(jnp.searchsorted(pend, jnp.arange(nb, dtype=jnp.int32) * r, side='right'),
                     0, n_exp - 1).astype(jnp.int32)
    n_used = (pend[-1:] // r).astype(jnp.int32)
    zpos = (pstart + counts).astype(jnp.int32)
    xs = _dispatch(h2, dest, zpos, (nb + 1) * r, n_exp)
    ys = _gmm(xs, blk_e, n_used, lp['exp_w_gate'].astype(BF16), lp['exp_w_up'].astype(BF16),
              lp['exp_w_down'].astype(BF16), nb)
    return _combine(ys, dest, gate_p, base, mods3, row_of_tile, lp['ln2_g'], lp['ln2_b'])


def _layer_pair(xp, xs, s_f, s_b, c, c_ctx, lp, alpha):
    bp, lp_len, d = xp.shape
    bs, ls_len, _ = xs.shape
    tp, ts = bp * lp_len, bs * ls_len
    t = tp + ts
    tm = 256
    assert lp_len % DFT_BLOCK == 0 and ls_len % DFT_BLOCK == 0 and DFT_BLOCK % (ls_len // DFT_BLOCK) == 0
    assert tp % ls_len == 0 and tp % MOE_ROWS == 0 and ts % MOE_ROWS == 0

    mod_rows = 8 * ((1 + bs + 7) // 8)
    cond = jnp.zeros((mod_rows, d), F32).at[0].set(c_ctx).at[1:1 + bs].set(c)
    mods3 = _ada(cond, lp['w_ada'], lp['b_ada']).reshape(mod_rows, 1, 6 * d)

    def row_of_tile(i, tile=tm):
        return jnp.where(i * tile < tp, 0, 1 + (i * tile - tp) // ls_len)

    x_all = jnp.concatenate([xp.reshape(tp, d), xs.reshape(ts, d)], axis=0)
    proj = _proj(x_all, mods3, row_of_tile, lp['w_in'].astype(BF16), tm)

    cw = lp['hy_bias'].shape[1]
    hy_cols = (HY_ORDER + 1) * cw
    y_hy = jnp.concatenate([_hyena(proj, 0, bp, lp_len, lp, cw),
                            _hyena(proj, tp // ls_len, bs, ls_len, lp, cw)], axis=0)

    lg = jnp.stack([jax.nn.log_sigmoid(lp['ret_decay_f'].astype(F32)),
                    jax.nn.log_sigmoid(lp['ret_decay_b'].astype(F32))])
    dk = s_f.shape[-1]
    zero_state = jnp.zeros((bp, RET_HEADS, dk, dk), F32)
    yr_p, nf, nb_ = _retention(proj, 0, bp, lp_len, hy_cols, lg, zero_state, zero_state, False)
    yr_s, _, _ = _retention(proj, tp // ls_len, bs, ls_len, hy_cols, lg, s_f, s_b, True)
    y_ret = jnp.concatenate([yr_p, yr_s], axis=0)

    base, h2, logits = _out_stage(y_hy, y_ret, x_all, mods3, row_of_tile, lp, alpha, tm)
    y_all = _moe(h2, logits, base, mods3, row_of_tile, lp)
    return y_all[:tp].reshape(bp, lp_len, d), y_all[tp:].reshape(bs, ls_len, d), nf, nb_


def kernel(x_prompt, x_sample, state_ret_fwd, state_ret_bwd, c, c_ctx, w_in, hy_conv_w, hy_conv_b, hy_ffn_w1, hy_ffn_b1, hy_ffn_w2, hy_ffn_b2, hy_ffn_w3, hy_sin_freq, hy_bias, ret_decay_f, ret_decay_b, w_out, w_ada, b_ada, ln1_g, ln1_b, ln2_g, ln2_b, router_w, router_b, exp_w_gate, exp_w_up, exp_w_down, sh_w_gate, sh_w_up, sh_w_down):
    depth = w_in.shape[0]
    alpha = (2.0 * depth) ** 0.25
    params = dict(w_in=w_in, hy_conv_w=hy_conv_w, hy_conv_b=hy_conv_b, hy_ffn_w1=hy_ffn_w1, hy_ffn_b1=hy_ffn_b1,
                  hy_ffn_w2=hy_ffn_w2, hy_ffn_b2=hy_ffn_b2, hy_ffn_w3=hy_ffn_w3, hy_sin_freq=hy_sin_freq,
                  hy_bias=hy_bias, ret_decay_f=ret_decay_f, ret_decay_b=ret_decay_b, w_out=w_out, w_ada=w_ada,
                  b_ada=b_ada, ln1_g=ln1_g, ln1_b=ln1_b, ln2_g=ln2_g, ln2_b=ln2_b, router_w=router_w,
                  router_b=router_b, exp_w_gate=exp_w_gate, exp_w_up=exp_w_up, exp_w_down=exp_w_down,
                  sh_w_gate=sh_w_gate, sh_w_up=sh_w_up, sh_w_down=sh_w_down)
    y_p, y_s = x_prompt, x_sample
    new_f, new_b = [], []
    for l in range(depth):
        lp = {k: v[l] for k, v in params.items()}
        y_p, y_s, s_f, s_b = _layer_pair(y_p, y_s, state_ret_fwd[:, l], state_ret_bwd[:, l], c, c_ctx, lp, alpha)
        new_f.append(s_f.astype(x_prompt.dtype))
        new_b.append(s_b.astype(x_prompt.dtype))
    return (y_p, y_s, jnp.stack(new_f, axis=1), jnp.stack(new_b, axis=1))
```

```python
import functools
import math

import numpy as np
import jax
import jax.numpy as jnp
from jax import lax
from jax.experimental import pallas as pl
from jax.experimental.pallas import tpu as pltpu

F32 = jnp.float32
BF16 = jnp.bfloat16

RET_HEADS = 8
TOP_K = 8
GRID_W = 64
HY_ORDER = 2
HY_BANDS = 16
HY_TARGET = 1e-2
HY_FAST = 0.3
HY_SLOW = 1.5
ROUTED_SCALE = 2.5
ROPE_BASE = 10000.0
LN_EPS = 1e-5
GN_EPS = 1e-6

V7X_LANES = 128
V7X_VMEM_LIMIT_BYTES = 56 * 1024 * 1024

DFT_BLOCK = 256
RET_CHUNK = 256
PROJ_ROWS = 1024
MOE_ROWS = 512
TOK_SUB = 16
GATHER_PITCH = 24


def _cparams(sem, vmem=V7X_VMEM_LIMIT_BYTES):
    return pltpu.CompilerParams(dimension_semantics=sem, vmem_limit_bytes=vmem)


def _dot(a, b):
    return jnp.dot(a, b, preferred_element_type=F32)


def _split_bf16(a):
    hi = a.astype(BF16)
    lo = (a - hi.astype(F32)).astype(BF16)
    return hi, lo


def _dot3(a, b):
    ah, al = _split_bf16(a)
    bh, bl = _split_bf16(b)
    return _dot(ah, bh) + _dot(ah, bl) + _dot(al, bh)


def _silu(x):
    return x * jax.nn.sigmoid(x)


def _whole(shape):
    n = len(shape)
    return pl.BlockSpec(shape, lambda *_: (0,) * n)


def _ada_kernel(c_ref, w_ref, b_ref, o_ref):
    o_ref[...] = _dot3(_silu(c_ref[...]), w_ref[...]) + b_ref[...]


def _ada(cond, w_ada, b_ada, tn=1024):
    rows, d = cond.shape
    n = w_ada.shape[1]
    return pl.pallas_call(
        _ada_kernel,
        grid=(n // tn,),
        in_specs=[pl.BlockSpec((rows, d), lambda j: (0, 0)),
                  pl.BlockSpec((d, tn), lambda j: (0, j)),
                  pl.BlockSpec((1, tn), lambda j: (0, j))],
        out_specs=pl.BlockSpec((rows, tn), lambda j: (0, j)),
        out_shape=jax.ShapeDtypeStruct((rows, n), F32),
        compiler_params=_cparams(("parallel",)),
        name='ada_mods',
    )(cond, w_ada, b_ada.reshape(1, n))


def _proj_kernel(x_ref, sc_ref, sh_ref, w_ref, o_ref, h_scr):
    @pl.when(pl.program_id(1) == 0)
    def _():
        h_scr[...] = (x_ref[...] * (1.0 + sc_ref[0]) + sh_ref[0]).astype(BF16)

    o_ref[...] = _dot(h_scr[...], w_ref[...]).astype(o_ref.dtype)


def _proj(x, mods3, row_of_tile, w_in_bf16, name, tm, tn=1024):
    t, d = x.shape
    n = w_in_bf16.shape[1]
    row_of_tile = functools.partial(row_of_tile, tile=tm)
    return pl.pallas_call(
        _proj_kernel,
        name=name,
        grid=(t // tm, n // tn),
        in_specs=[pl.BlockSpec((tm, d), lambda i, j: (i, 0)),
                  pl.BlockSpec((1, 1, d), lambda i, j: (row_of_tile(i), 0, 1)),
                  pl.BlockSpec((1, 1, d), lambda i, j: (row_of_tile(i), 0, 0)),
                  pl.BlockSpec((d, tn), lambda i, j: (0, j))],
        out_specs=pl.BlockSpec((tm, tn), lambda i, j: (i, j)),
        out_shape=jax.ShapeDtypeStruct((t, n), BF16),
        scratch_shapes=[pltpu.VMEM((tm, d), BF16)],
        compiler_params=_cparams(("parallel", "arbitrary")),
    )(x, mods3, mods3, w_in_bf16)


@functools.lru_cache(maxsize=None)
def _hy_mats(L):
    nn2 = L // DFT_BLOCK
    n_full = 2 * L
    n2_full = 2 * nn2
    g_rows = DFT_BLOCK // nn2
    k2 = np.arange(nn2)
    n2 = np.arange(nn2)
    s1 = np.zeros((nn2, DFT_BLOCK, DFT_BLOCK), np.complex128)
    for g in range(nn2):
        for s in range(g_rows):
            n1 = g * g_rows + s
            blk = (np.exp(-2j * np.pi * np.outer(k2 + 0.5, n2) / n2_full)
                   * np.exp(-2j * np.pi * (k2 + 0.5) * n1 / n_full)[:, None])
            s1[g][np.ix_(k2 * g_rows + s, n2 * g_rows + s)] = blk
    m1f = np.concatenate([s1.real, s1.imag], axis=1)
    a = np.conj(s1).transpose(0, 2, 1)
    m1i = (2.0 / n_full) * np.concatenate([a.real, -a.imag], axis=2)
    k1 = np.arange(DFT_BLOCK)
    f = np.exp(-2j * np.pi * np.outer(k1, k1) / DFT_BLOCK)
    f2f = np.block([[f.real, -f.imag], [f.imag, f.real]])
    f2i = np.block([[f.real, f.imag], [-f.imag, f.real]])
    return tuple(np.asarray(m, np.float32) for m in (m1f, m1i, f2f, f2i))


def _hy_unroll(nn2):
    return 4 if nn2 % 4 == 0 else 1


def _hy_fwd_stage1(u_scr, m1f_ref, v_scr, nn2):
    g_rows = DFT_BLOCK // nn2

    def body(g, carry):
        off = pl.multiple_of(g * g_rows, g_rows)
        parts = [u_scr[pl.ds(n2 * DFT_BLOCK + off, g_rows), :] for n2 in range(nn2)]
        inp = jnp.concatenate(parts, axis=0).astype(BF16)
        out = _dot(m1f_ref[g], inp)
        for ri in range(2):
            for k2 in range(nn2):
                r0 = (ri * nn2 + k2) * g_rows
                v_scr[k2, pl.ds(ri * DFT_BLOCK + off, g_rows), :] = out[r0:r0 + g_rows].astype(BF16)
        return carry

    lax.fori_loop(0, nn2, body, 0, unroll=2 if nn2 % 2 == 0 else 1)


def _hy_inv_stage1(w_scr, m1i_ref, y_scr, nn2):
    g_rows = DFT_BLOCK // nn2

    def body(g, carry):
        off = pl.multiple_of(g * g_rows, g_rows)
        parts = [w_scr[k2, pl.ds(ri * DFT_BLOCK + off, g_rows), :]
                 for ri in range(2) for k2 in range(nn2)]
        inp = jnp.concatenate(parts, axis=0)
        out = _dot(m1i_ref[g], inp)
        for n2 in range(nn2):
            y_scr[pl.ds(n2 * DFT_BLOCK + off, g_rows), :] = out[n2 * g_rows:(n2 + 1) * g_rows]
        return carry

    lax.fori_loop(0, nn2, body, 0, unroll=2 if nn2 % 2 == 0 else 1)


def _short_conv(u, w, b):
    L = u.shape[0]
    row = lax.broadcasted_iota(jnp.int32, u.shape, 0)
    prev = jnp.where(row == 0, 0.0, pltpu.roll(u, 1, 0))
    nxt = jnp.where(row == L - 1, 0.0, pltpu.roll(u, L - 1, 0))
    return prev * w[0:1] + u * w[1:2] + nxt * w[2:3] + b


def _hy_filter_kernel(z_ref, w1_ref, b1_ref, w2_ref, b2_ref, w3_ref, fr_ref, dl_ref, o_ref, *, n_half_tiles, c):
    i = pl.program_id(0)
    z = z_ref[...]
    fr = fr_ref[...]
    h = jnp.sin(fr * (_dot3(z, w1_ref[...]) + b1_ref[...]))
    h = jnp.sin(fr * (_dot3(h, w2_ref[...]) + b2_ref[...]))
    h = _dot3(h, w3_ref[...])
    tn = z[:, 0:1]
    decay = jnp.exp(-tn * jnp.abs(dl_ref[...]))
    valid = z[:, 2 * HY_BANDS + 1:2 * HY_BANDS + 2]
    second = i >= n_half_tiles
    for o in range(HY_ORDER):
        fwd = h[:, (2 * o) * c:(2 * o + 1) * c]
        bwd = h[:, (2 * o + 1) * c:(2 * o + 2) * c]
        val = jnp.where(second, -bwd, fwd) * decay * valid
        o_ref[:, o * c:(o + 1) * c] = val


def _hy_filters(L, w1, b1, w2, b2, w3, freq, name, rt=256):
    c = w3.shape[1] // (2 * HY_ORDER)
    fh = w1.shape[1]
    r = np.arange(2 * L)
    t = np.where(r < L, r, 2 * L - r).astype(np.float64)
    f = np.linspace(1e-4, HY_BANDS - 1, HY_BANDS)
    ang = (2.0 * math.pi / L) * t[:, None] * f[None, :]
    z = np.zeros((2 * L, V7X_LANES), np.float32)
    z[:, 0] = t / (L - 1)
    z[:, 1:1 + HY_BANDS] = np.cos(ang)
    z[:, 1 + HY_BANDS:1 + 2 * HY_BANDS] = -np.sin(ang)
    z[:, 1 + 2 * HY_BANDS] = (r != L)
    w1p = jnp.zeros((V7X_LANES, fh), F32).at[:w1.shape[0]].set(w1)
    deltas = np.linspace(math.log(HY_TARGET) / HY_SLOW, math.log(HY_TARGET) / HY_FAST, c).astype(np.float32)
    kern = functools.partial(_hy_filter_kernel, n_half_tiles=L // rt, c=c)
    return pl.pallas_call(
        kern,
        grid=(2 * L // rt,),
        in_specs=[pl.BlockSpec((rt, V7X_LANES), lambda i: (i, 0)),
                  _whole((V7X_LANES, fh)), _whole((1, fh)), _whole((fh, fh)), _whole((1, fh)),
                  _whole(w3.shape), _whole((1, fh)), _whole((1, c))],
        out_specs=pl.BlockSpec((rt, HY_ORDER * c), lambda i: (i, 0)),
        out_shape=jax.ShapeDtypeStruct((2 * L, HY_ORDER * c), F32),
        compiler_params=_cparams(("parallel",)),
        name=name,
    )(jnp.asarray(z), w1p, b1.reshape(1, fh), w2, b2.reshape(1, fh), w3, freq.reshape(1, fh),
      jnp.asarray(deltas).reshape(1, c))


def _hy_spec_kernel(f_ref, m1f_ref, f2f_ref, o_ref, u_scr, v_scr, acc_scr, *, nn2):
    half = pl.program_id(1)
    u_scr[...] = f_ref[...]
    _hy_fwd_stage1(u_scr, m1f_ref, v_scr, nn2)

    def body(k2, carry):
        z = _dot(f2f_ref[...], v_scr[k2])

        @pl.when(half == 0)
        def _():
            acc_scr[k2] = z

        @pl.when(half == 1)
        def _():
            sign = 1.0 - 2.0 * (k2 % 2)
            a = acc_scr[k2]
            o_ref[0, k2] = (a[:DFT_BLOCK] + sign * z[DFT_BLOCK:]).astype(o_ref.dtype)
            o_ref[1, k2] = (a[DFT_BLOCK:] - sign * z[:DFT_BLOCK]).astype(o_ref.dtype)

        return carry

    lax.fori_loop(0, nn2, body, 0, unroll=2 if nn2 % 2 == 0 else 1)


def _hy_spectrum(filt, L, name, ct=256):
    nn2 = L // DFT_BLOCK
    cols = filt.shape[1]
    m1f, _, f2f, _ = _hy_mats(L)
    kern = functools.partial(_hy_spec_kernel, nn2=nn2)
    return pl.pallas_call(
        kern,
        grid=(cols // ct, 2),
        in_specs=[pl.BlockSpec((L, ct), lambda j, h: (h, j)),
                  pl.BlockSpec(memory_space=pltpu.VMEM),
                  pl.BlockSpec(memory_space=pltpu.VMEM)],
        out_specs=pl.BlockSpec((2, nn2, DFT_BLOCK, ct), lambda j, h: (0, 0, 0, j)),
        out_shape=jax.ShapeDtypeStruct((2, nn2, DFT_BLOCK, cols), BF16),
        scratch_shapes=[pltpu.VMEM((L, ct), F32),
                        pltpu.VMEM((nn2, 2 * DFT_BLOCK, ct), BF16),
                        pltpu.VMEM((nn2, 2 * DFT_BLOCK, ct), F32)],
        compiler_params=_cparams(("parallel", "arbitrary")),
        name=name,
    )(filt, jnp.asarray(m1f, BF16), jnp.asarray(f2f, BF16))


def _hy_conv_kernel(u_ref, g_ref, cwu_ref, cbu_ref, cwg_ref, cbg_ref, bias_ref, ks_ref,
                    m1f_ref, m1i_ref, f2f_ref, f2i_ref, o_ref, u_scr, v_scr, w_scr, y_scr, *, nn2, conv_u):
    u = u_ref[...].astype(F32)
    if conv_u:
        u = _short_conv(u, cwu_ref[...], cbu_ref[...])
    u_scr[...] = u
    _hy_fwd_stage1(u_scr, m1f_ref, v_scr, nn2)

    def body(k2, carry):
        z = _dot(f2f_ref[...], v_scr[k2])
        zr, zi = z[:DFT_BLOCK], z[DFT_BLOCK:]
        kr = ks_ref[0, k2].astype(F32)
        ki = ks_ref[1, k2].astype(F32)
        y = jnp.concatenate([zr * kr - zi * ki, zr * ki + zi * kr], axis=0).astype(BF16)
        w_scr[k2] = _dot(f2i_ref[...], y).astype(BF16)
        return carry

    lax.fori_loop(0, nn2, body, 0, unroll=_hy_unroll(nn2))
    _hy_inv_stage1(w_scr, m1i_ref, y_scr, nn2)
    gate = _short_conv(g_ref[...].astype(F32), cwg_ref[...], cbg_ref[...])
    o_ref[...] = (gate * (y_scr[...] + bias_ref[...] * u_scr[...])).astype(o_ref.dtype)


def _hy_conv(u_arr, u_col0, conv_u, proj, g_col0, n_seq, L, conv_w, conv_b, bias_o, spec, spec_col0, c, name, ct=256):
    nn2 = L // DFT_BLOCK
    m1f, m1i, f2f, f2i = (jnp.asarray(m, BF16) for m in _hy_mats(L))
    nct = c // ct
    ub, gb, sb = u_col0 // ct, g_col0 // ct, spec_col0 // ct
    kern = functools.partial(_hy_conv_kernel, nn2=nn2, conv_u=conv_u)
    vm = pl.BlockSpec(memory_space=pltpu.VMEM)
    return pl.pallas_call(
        kern,
        grid=(nct, n_seq),
        in_specs=[pl.BlockSpec((L, ct), lambda j, b: (b, ub + j)),
                  pl.BlockSpec((L, ct), lambda j, b: (b, gb + j)),
                  pl.BlockSpec((3, ct), lambda j, b: (0, ub + j)),
                  pl.BlockSpec((1, ct), lambda j, b: (0, ub + j)),
                  pl.BlockSpec((3, ct), lambda j, b: (0, gb + j)),
                  pl.BlockSpec((1, ct), lambda j, b: (0, gb + j)),
                  pl.BlockSpec((1, ct), lambda j, b: (0, j)),
                  pl.BlockSpec((2, nn2, DFT_BLOCK, ct), lambda j, b: (0, 0, 0, sb + j)),
                  vm, vm, vm, vm],
        out_specs=pl.BlockSpec((L, ct), lambda j, b: (b, j)),
        out_shape=jax.ShapeDtypeStruct((n_seq * L, c), BF16),
        scratch_shapes=[pltpu.VMEM((L, ct), F32),
                        pltpu.VMEM((nn2, 2 * DFT_BLOCK, ct), BF16),
                        pltpu.VMEM((nn2, 2 * DFT_BLOCK, ct), BF16),
                        pltpu.VMEM((L, ct), F32)],
        compiler_params=_cparams(("parallel", "parallel")),
        name=name,
    )(u_arr, proj, conv_w, conv_b, conv_w, conv_b, bias_o, spec, m1f, m1i, f2f, f2i)


def _hyena(proj, n_seq, L, lp, c, tag):
    filt = _hy_filters(L, lp['hy_ffn_w1'], lp['hy_ffn_b1'], lp['hy_ffn_w2'], lp['hy_ffn_b2'],
                       lp['hy_ffn_w3'], lp['hy_sin_freq'], 'hy_filter_' + tag)
    spec = _hy_spectrum(filt, L, 'hy_spectrum_' + tag)
    cw = lp['hy_conv_w']
    cb = lp['hy_conv_b'].reshape(1, -1)
    bias = lp['hy_bias']
    z = _hy_conv(proj, 0, True, proj, c, n_seq, L, cw, cb, bias[0:1], spec, 0, c, 'hy_conv1_' + tag)
    z = _hy_conv(z, 0, False, proj, 2 * c, n_seq, L, cw, cb, bias[1:2], spec, c, c, 'hy_conv2_' + tag)
    return z


def _rope_tables(L, dk):
    rows = L // GRID_W
    row = np.repeat(np.arange(rows), GRID_W).astype(np.float64)
    col = np.tile(np.arange(GRID_W), rows).astype(np.float64)
    quarter = dk // 4
    inv = ROPE_BASE ** (-np.arange(quarter, dtype=np.float64) / quarter)
    ar = row[:, None] * inv[None, :]
    ac = col[:, None] * inv[None, :]
    cos = np.concatenate([np.cos(ar), np.cos(ar), np.cos(ac), np.cos(ac)], axis=1)
    sin = np.concatenate([-np.sin(ar), np.sin(ar), -np.sin(ac), np.sin(ac)], axis=1)
    return jnp.asarray(cos, F32), jnp.asarray(sin, F32)


def _rope(x, cos, sin_signed):
    dk = x.shape[1]
    q = dk // 4
    lane = lax.broadcasted_iota(jnp.int32, x.shape, 1)
    first = (lane % (2 * q)) < q
    swapped = jnp.where(first, pltpu.roll(x, dk - q, 1), pltpu.roll(x, q, 1))
    return x * cos + swapped * sin_signed


def _ret_kernel(lg_ref, q_ref, k_ref, v_ref, g_ref, cos_ref, sin_ref, s0f_ref, s0b_ref,
                y_ref, sf_ref, sb_ref, qd_scr, u_scr, s_scr, o_scr, *, latent, chunk):
    h = pl.program_id(1)
    L, dk = q_ref.shape
    n = L // chunk
    lgf = lg_ref[0, h]
    lgb = lg_ref[1, h]
    pos = lax.broadcasted_iota(jnp.int32, (chunk, dk), 0).astype(F32)
    qd_f = jnp.exp(lgf * (pos + 1.0))
    qd_b = jnp.exp(lgb * (chunk - pos))
    kd_f = jnp.exp(lgf * (chunk - 1.0 - pos))
    kd_b = jnp.exp(lgb * pos)
    pos_r = lax.broadcasted_iota(jnp.int32, (chunk, chunk), 0).astype(F32)
    pos_c = lax.broadcasted_iota(jnp.int32, (chunk, chunk), 1).astype(F32)
    diff = pos_r - pos_c
    mask = (jnp.where(diff >= 0, jnp.exp(lgf * jnp.maximum(diff, 0.0)), 0.0)
            + jnp.where(diff <= 0, jnp.exp(lgb * jnp.maximum(-diff, 0.0)), 0.0))
    unroll = 2 if n % 2 == 0 else 1

    def intra(i, carry):
        r0 = pl.multiple_of(i * chunk, chunk)
        rows = pl.ds(r0, chunk)
        q = q_ref[rows, :].astype(F32)
        k = k_ref[rows, :].astype(F32) * (dk ** -0.5)
        if latent:
            q = _rope(q, cos_ref[rows, :], sin_ref[rows, :])
            k = _rope(k, cos_ref[rows, :], sin_ref[rows, :])
        qd_scr[rows, :dk] = (q * qd_f).astype(BF16)
        qd_scr[rows, dk:] = (q * qd_b).astype(BF16)
        vi = v_ref[rows, :]
        att = lax.dot_general(q.astype(BF16), k.astype(BF16), (((1,), (1,)), ((), ())),
                              preferred_element_type=F32) * mask
        o_scr[rows, :] = _dot(att.astype(BF16), vi)
        kd = jnp.concatenate([k * kd_f, k * kd_b], axis=1)
        u_scr[i] = _dot(kd.T.astype(BF16), vi)
        return carry

    lax.fori_loop(0, n, intra, 0, unroll=unroll)

    c_f = jnp.exp(lgf * chunk)
    c_b = jnp.exp(lgb * chunk)

    def fwd_state(i, s):
        s_scr[i, :dk, :] = s.astype(BF16)
        return c_f * s + u_scr[i, :dk, :]

    def bwd_state(j, s):
        i = n - 1 - j
        s_scr[i, dk:, :] = s.astype(BF16)
        return c_b * s + u_scr[i, dk:, :]

    sf_ref[0, 0] = lax.fori_loop(0, n, fwd_state, s0f_ref[0, 0])
    sb_ref[0, 0] = lax.fori_loop(0, n, bwd_state, s0b_ref[0, 0])

    def inter(i, carry):
        rows = pl.ds(pl.multiple_of(i * chunk, chunk), chunk)
        o = o_scr[rows, :] + _dot(qd_scr[rows, :], s_scr[i])
        mu = jnp.mean(o, axis=-1, keepdims=True)
        var = jnp.mean(jnp.square(o - mu), axis=-1, keepdims=True)
        o = (o - mu) * lax.rsqrt(var + GN_EPS)
        y_ref[rows, :] = (o * _silu(g_ref[rows, :].astype(F32))).astype(y_ref.dtype)
        return carry

    lax.fori_loop(0, n, inter, 0, unroll=unroll)


def _retention(proj, n_seq, L, col0, lg, s0_f, s0_b, latent, name):
    dk = s0_f.shape[-1]
    hb = col0 // dk
    chunk = min(RET_CHUNK, L)
    cos, sin = _rope_tables(L, dk) if latent else (jnp.zeros((8, dk), F32), jnp.zeros((8, dk), F32))
    tbl = pl.BlockSpec(cos.shape, lambda b, h: (0, 0))
    seq = lambda part: pl.BlockSpec((L, dk), lambda b, h: (b, hb + part * RET_HEADS + h))
    st = pl.BlockSpec((1, 1, dk, dk), lambda b, h: (b, h, 0, 0))
    kern = functools.partial(_ret_kernel, latent=latent, chunk=chunk)
    return pl.pallas_call(
        kern,
        grid=(n_seq, RET_HEADS),
        in_specs=[pl.BlockSpec(memory_space=pltpu.SMEM),
                  seq(0), seq(1), seq(2), seq(3), tbl, tbl, st, st],
        out_specs=[pl.BlockSpec((L, dk), lambda b, h: (b, h)), st, st],
        out_shape=[jax.ShapeDtypeStruct((n_seq * L, RET_HEADS * dk), BF16),
                   jax.ShapeDtypeStruct((n_seq, RET_HEADS, dk, dk), F32),
                   jax.ShapeDtypeStruct((n_seq, RET_HEADS, dk, dk), F32)],
        scratch_shapes=[pltpu.VMEM((L, 2 * dk), BF16),
                        pltpu.VMEM((L // chunk, 2 * dk, dk), F32), pltpu.VMEM((L // chunk, 2 * dk, dk), BF16),
                        pltpu.VMEM((L, dk), F32)],
        compiler_params=_cparams(("parallel", "parallel")),
        name=name,
    )(lg, proj, proj, proj, proj, cos, sin, s0_f, s0_b)


def _layernorm(r, g, b):
    mu = jnp.mean(r, axis=-1, keepdims=True)
    var = jnp.mean(jnp.square(r - mu), axis=-1, keepdims=True)
    return (r - mu) * lax.rsqrt(var + LN_EPS) * g + b


def _route(logits, router_b, tri, before0, n_exp):
    tm, ep = logits.shape
    lane = lax.broadcasted_iota(jnp.int32, (tm, ep), 1).astype(F32)
    scores = jax.nn.sigmoid(logits)
    work = jnp.where(lane < n_exp, scores + router_b, -jnp.inf)
    sel = jnp.zeros((tm, ep), jnp.bool_)
    idx_cols = []
    s_cols = []
    for _ in range(TOP_K):
        m = jnp.max(work, axis=-1, keepdims=True)
        first = jnp.min(jnp.where(work == m, lane, float(ep)), axis=-1, keepdims=True)
        hit = lane == first
        idx_cols.append(first)
        s_cols.append(jnp.sum(jnp.where(hit, scores, 0.0), axis=-1, keepdims=True))
        sel = jnp.logical_or(sel, hit)
        work = jnp.where(hit, -jnp.inf, work)
    denom = functools.reduce(lambda a, b: a + b, s_cols)
    before = _dot(tri, sel.astype(BF16)) + before0
    idx_out = jnp.zeros((tm, ep), F32)
    gate_out = jnp.zeros((tm, ep), F32)
    rank_out = jnp.zeros((tm, ep), jnp.int32)
    for j in range(TOP_K):
        hit = lane == idx_cols[j]
        rank_j = jnp.sum(jnp.where(hit, before, 0.0), axis=-1, keepdims=True).astype(jnp.int32)
        idx_out = jnp.where(lane == j, idx_cols[j], idx_out)
        gate_out = jnp.where(lane == j, ROUTED_SCALE * s_cols[j] / denom, gate_out)
        rank_out = jnp.where(lane == j, rank_j, rank_out)
    counts = before0 + jnp.sum(sel.astype(F32), axis=0, keepdims=True)
    return idx_out.astype(jnp.int32), gate_out, rank_out, counts


def _out_kernel(yh_ref, yr_ref, x_ref, g1_ref, sc2_ref, sh2_ref, g2_ref, lng_ref, lnb_ref,
                wo_ref, wg_ref, wu_ref, wd_ref, rwh_ref, rwl_ref, rb_ref, tri_ref, cnt0_ref,
                base_ref, h2_ref, idx_ref, gate_ref, rank_ref, cnt_ref, *, alpha, n_exp):
    @pl.when(pl.program_id(0) == 0)
    def _():
        cnt_ref[...] = cnt0_ref[...]

    c = yh_ref.shape[1]
    mix = _dot(yh_ref[...], wo_ref[:c]) + _dot(yr_ref[...], wo_ref[c:])
    x1 = _layernorm(alpha * x_ref[...] + g1_ref[0] * mix, lng_ref[...], lnb_ref[...])
    h2 = x1 * (1.0 + sc2_ref[0]) + sh2_ref[0]
    _to_token_tiles(h2_ref, h2)
    hh, hl = _split_bf16(h2)
    logits = _dot(hh, rwh_ref[...]) + _dot(hh, rwl_ref[...]) + _dot(hl, rwh_ref[...])
    a = (_silu(_dot(hh, wg_ref[...])) * _dot(hh, wu_ref[...])).astype(BF16)
    base_ref[...] = alpha * x1 + g2_ref[0] * _dot(a, wd_ref[...])
    idx_ref[...], gate_ref[...], rank_ref[...], cnt_ref[...] = _route(
        logits, rb_ref[...], tri_ref[...], cnt_ref[...], n_exp)


def _out_stage(y_hy, y_ret, x, mods3, row_of_tile, wts, cnt0, alpha, n_exp, name, tm=256):
    t, d = x.shape
    c = y_hy.shape[1]
    wo, sg, su, sd, rwh, rwl, ln_g, ln_b, rb = wts
    ep = rwh.shape[1]
    tri = jnp.asarray(np.tril(np.ones((tm, tm), np.float32), -1), BF16)
    row_of_tile = functools.partial(row_of_tile, tile=tm)
    mod = lambda k: pl.BlockSpec((1, 1, d), lambda i: (row_of_tile(i), 0, k))
    vm = pl.BlockSpec(memory_space=pltpu.VMEM)
    tile = lambda w: pl.BlockSpec((tm, w), lambda i: (i, 0))
    kern = functools.partial(_out_kernel, alpha=alpha, n_exp=n_exp)
    return pl.pallas_call(
        kern,
        grid=(t // tm,),
        in_specs=[tile(c), tile(c), tile(d), mod(2), mod(4), mod(3), mod(5),
                  _whole((1, d)), _whole((1, d)), vm, vm, vm, vm, vm, vm, vm, vm, vm],
        out_specs=[tile(d), pl.BlockSpec((tm * TOK_SUB, d // TOK_SUB), lambda i: (i, 0)),
                   tile(ep), tile(ep), tile(ep), _whole((1, ep))],
        out_shape=[jax.ShapeDtypeStruct((t, d), F32), jax.ShapeDtypeStruct((t * TOK_SUB, d // TOK_SUB), F32),
                   jax.ShapeDtypeStruct((t, ep), jnp.int32), jax.ShapeDtypeStruct((t, ep), F32),
                   jax.ShapeDtypeStruct((t, ep), jnp.int32), jax.ShapeDtypeStruct((1, ep), F32)],
        compiler_params=_cparams(("arbitrary",)),
        name=name,
    )(y_hy, y_ret, x, mods3, mods3, mods3, mods3, ln_g.reshape(1, d), ln_b.reshape(1, d),
      wo, sg, su, sd, rwh, rwl, rb, tri, cnt0)


def _to_token_tiles(ref, val):
    rows, d = val.shape
    w = d // TOK_SUB
    for s in range(TOK_SUB):
        ref[pl.ds(s, rows, stride=TOK_SUB), :] = val[:, s * w:(s + 1) * w]


def _from_token_tiles(ref, row0, rows):
    return jnp.concatenate([ref[pl.ds(row0 * TOK_SUB + s, rows, stride=TOK_SUB), :] for s in range(TOK_SUB)],
                           axis=1)


def _row_copy(src_ref, src_row, dst_ref, dst_row, sem):
    src = src_ref.at[pl.ds(pl.multiple_of(src_row * TOK_SUB, TOK_SUB), TOK_SUB), :]
    dst = dst_ref.at[pl.ds(pl.multiple_of(dst_row * TOK_SUB, TOK_SUB), TOK_SUB), :]
    return pltpu.make_async_copy(src, dst, sem)


def _dispatch_kernel(dest_ref, zpos_ref, hc_ref, hl_ref, xs_ref, zero_scr, sem, *, n_exp, tm, ctx_tiles, nb):
    i = pl.program_id(0)

    @pl.when(i == 0)
    def _():
        zero_scr[...] = jnp.zeros_like(zero_scr)
        max_piece = zero_scr.shape[0] // TOK_SUB
        pieces = [max_piece >> k for k in range(max_piece.bit_length())]

        def zero_copy(slot, n_tok):
            dst = xs_ref.at[pl.ds(pl.multiple_of(slot * TOK_SUB, TOK_SUB), n_tok * TOK_SUB), :]
            return pltpu.make_async_copy(zero_scr.at[pl.ds(0, n_tok * TOK_SUB), :], dst, sem)

        def each_pad_piece(e, fn):
            pad = zpos_ref[n_exp + e]
            for b in pieces:
                @pl.when((pad & b) != 0)
                def _():
                    fn(zero_copy(zpos_ref[e] + (pad & (-2 * b)), b))

        def each_tail_piece(b, fn):
            for half in range(MOE_ROWS // max_piece):
                fn(zero_copy(b * MOE_ROWS + half * max_piece, max_piece))

        def run(fn):
            lax.fori_loop(0, n_exp, lambda e, c: (each_pad_piece(e, fn), c)[1], 0)
            lax.fori_loop(zpos_ref[2 * n_exp], nb, lambda b, c: (each_tail_piece(b, fn), c)[1], 0)

        run(lambda cp: cp.start())
        run(lambda cp: cp.wait())

    def scatter(h_ref):
        def start(t, c):
            for j in range(TOP_K):
                _row_copy(h_ref, t, xs_ref, dest_ref[t * TOP_K + j], sem).start(priority=j % 2)
            return c

        lax.fori_loop(0, tm, start, 0)
        for _ in range(TOP_K):
            pltpu.make_async_copy(h_ref, xs_ref.at[pl.ds(0, tm * TOK_SUB), :], sem).wait()

    @pl.when(i < ctx_tiles)
    def _():
        scatter(hc_ref)

    @pl.when(i >= ctx_tiles)
    def _():
        scatter(hl_ref)


def _dispatch(h2t_ctx, h2t_lat, dest_flat, zpos, nb, n_exp, tm=256):
    w = h2t_ctx.shape[1]
    ctx_tiles = h2t_ctx.shape[0] // TOK_SUB // tm
    lat_tiles = h2t_lat.shape[0] // TOK_SUB // tm
    kern = functools.partial(_dispatch_kernel, n_exp=n_exp, tm=tm, ctx_tiles=ctx_tiles, nb=nb)
    return pl.pallas_call(
        kern,
        grid=(ctx_tiles + lat_tiles,),
        in_specs=[pl.BlockSpec((tm * TOP_K,), lambda i: (i,), memory_space=pltpu.SMEM),
                  pl.BlockSpec(memory_space=pltpu.SMEM),
                  pl.BlockSpec((tm * TOK_SUB, w), lambda i: (jnp.minimum(i, ctx_tiles - 1), 0)),
                  pl.BlockSpec((tm * TOK_SUB, w), lambda i: (jnp.maximum(i - ctx_tiles, 0), 0))],
        out_specs=pl.BlockSpec(memory_space=pl.ANY),
        out_shape=jax.ShapeDtypeStruct((nb * MOE_ROWS * TOK_SUB, w), F32),
        scratch_shapes=[pltpu.VMEM((MOE_ROWS // 2 * TOK_SUB, w), F32), pltpu.SemaphoreType.DMA(())],
        compiler_params=_cparams(("arbitrary",)),
        name='moe_dispatch',
    )(dest_flat, zpos, h2t_ctx, h2t_lat)


def _gmm_kernel(be_ref, nu_ref, x_ref, wg_ref, wu_ref, wd_ref, o_ref, wg_scr, wu_scr, wd_scr, y_scr):
    b = pl.program_id(0)
    rows = x_ref.shape[0] // TOK_SUB
    n_used = nu_ref[0]

    @pl.when(b == 0)
    def _():
        y_scr[...] = jnp.zeros_like(y_scr)

    @pl.when(b < n_used)
    def _():
        @pl.when(jnp.logical_or(b == 0, be_ref[b] != be_ref[jnp.maximum(b - 1, 0)]))
        def _():
            wg_scr[...] = wg_ref[0].astype(BF16)
            wu_scr[...] = wu_ref[0].astype(BF16)
            wd_scr[...] = wd_ref[0].astype(BF16)

        _to_token_tiles(o_ref, y_scr[...])
        x = _from_token_tiles(x_ref, 0, rows).astype(BF16)
        a = (_silu(_dot(x, wg_scr[...])) * _dot(x, wu_scr[...])).astype(BF16)
        y_scr[...] = _dot(a, wd_scr[...])

    @pl.when(b == n_used)
    def _():
        _to_token_tiles(o_ref, y_scr[...])

    @pl.when(b > n_used)
    def _():
        o_ref[...] = jnp.zeros_like(o_ref)


def _gmm(xs, blk_e, n_used, wg, wu, wd, nb):
    w = xs.shape[1]
    d, f = wg.shape[1], wg.shape[2]
    r8 = MOE_ROWS * TOK_SUB
    last = lambda b, nu: jnp.minimum(b, nu[0] - 1)
    grid_spec = pltpu.PrefetchScalarGridSpec(
        num_scalar_prefetch=2,
        grid=(nb + 1,),
        in_specs=[pl.BlockSpec((r8, w), lambda b, be, nu: (last(b, nu), 0)),
                  pl.BlockSpec((1, d, f), lambda b, be, nu: (be[last(b, nu)], 0, 0)),
                  pl.BlockSpec((1, d, f), lambda b, be, nu: (be[last(b, nu)], 0, 0)),
                  pl.BlockSpec((1, f, d), lambda b, be, nu: (be[last(b, nu)], 0, 0))],
        out_specs=pl.BlockSpec((r8, w), lambda b, be, nu: (jnp.maximum(b - 1, 0), 0)),
        scratch_shapes=[pltpu.VMEM((d, f), BF16), pltpu.VMEM((d, f), BF16), pltpu.VMEM((f, d), BF16),
                        pltpu.VMEM((MOE_ROWS, d), F32)],
    )
    return pl.pallas_call(
        _gmm_kernel,
        grid_spec=grid_spec,
        out_shape=jax.ShapeDtypeStruct((nb * r8, w), F32),
        compiler_params=_cparams(("arbitrary",)),
        name='moe_experts',
    )(blk_e, n_used, xs, wg, wu, wd)


def _combine_kernel(dest_ref, destn_ref, ys_ref, gate_ref, base_ref, g2_ref, lng_ref, lnb_ref, o_ref,
                    rows_scr, sem, *, tm, n_tiles):
    i = pl.program_id(0)
    slot_tok = TOP_K * tm
    slot_rows = slot_tok * TOK_SUB
    pitch = GATHER_PITCH

    def gather(d_ref, slot, t, j):
        src = ys_ref.at[pl.ds(pl.multiple_of(d_ref[t * TOP_K + j] * TOK_SUB, TOK_SUB), TOK_SUB), :]
        dst = rows_scr.at[pl.ds(pl.multiple_of((slot * slot_tok + j * tm + t) * pitch, 8), TOK_SUB), :]
        return pltpu.make_async_copy(src, dst, sem.at[slot])

    def issue(d_ref, slot):
        def start(t, c):
            for j in range(TOP_K):
                gather(d_ref, slot, t, j).start(priority=j % 2)
            return c

        lax.fori_loop(0, tm, start, 0)

    def wait_slot(slot):
        pltpu.make_async_copy(ys_ref.at[pl.ds(0, slot_rows), :],
                              rows_scr.at[pl.ds(slot * slot_tok * pitch, slot_rows), :], sem.at[slot]).wait()

    def finish(slot):
        wait_slot(slot)
        d = o_ref.shape[1]
        w = d // TOK_SUB
        tg = 32

        def group(gi, carry):
            g0 = pl.multiple_of(gi * tg, tg)
            grp = pl.ds(g0, tg)
            for tt in range(tg):
                for j in range(TOP_K):
                    gather(destn_ref, 1 - slot, g0 + tt, j).start(priority=j % 2)
            gates = gate_ref[grp, :]
            gb = [jnp.broadcast_to(gates[:, j:j + 1], (tg, w)) for j in range(TOP_K)]
            for s in range(TOK_SUB):
                cols = slice(s * w, (s + 1) * w)
                acc = None
                for j in range(TOP_K):
                    row0 = (slot * slot_tok + j * tm + g0) * pitch + s
                    term = rows_scr[pl.ds(row0, tg, stride=pitch), :] * gb[j]
                    acc = term if acc is None else acc + term
                o_ref[grp, cols] = base_ref[grp, cols] + g2_ref[0, :, cols] * acc
            return carry

        lax.fori_loop(0, tm // tg, group, 0)
        total = o_ref[:, :w]
        for s in range(1, TOK_SUB):
            total = total + o_ref[:, s * w:(s + 1) * w]
        mu = jnp.sum(total, axis=-1, keepdims=True) / d
        sq = jnp.square(o_ref[:, :w] - mu)
        for s in range(1, TOK_SUB):
            sq = sq + jnp.square(o_ref[:, s * w:(s + 1) * w] - mu)
        rstd = lax.rsqrt(jnp.sum(sq, axis=-1, keepdims=True) / d + LN_EPS)
        for s in range(TOK_SUB):
            cols = slice(s * w, (s + 1) * w)
            o_ref[:, cols] = (o_ref[:, cols] - mu) * rstd * lng_ref[:, cols] + lnb_ref[:, cols]

    @pl.when(i == 0)
    def _():
        issue(dest_ref, 0)

    for slot in range(2):
        @pl.when(i % 2 == slot)
        def _():
            finish(slot)

            @pl.when(i == n_tiles - 1)
            def _():
                wait_slot(1 - slot)


def _combine(ys, dest_flat, gates, base, mods3, row_of_tile, ln_g, ln_b, name, tm=128):
    t, d = base.shape
    ep = gates.shape[1]
    n_tiles = t // tm
    kern = functools.partial(_combine_kernel, tm=tm, n_tiles=n_tiles)
    return pl.pallas_call(
        kern,
        grid=(n_tiles,),
        in_specs=[pl.BlockSpec((tm * TOP_K,), lambda i: (i,), memory_space=pltpu.SMEM),
                  pl.BlockSpec((tm * TOP_K,), lambda i: (jnp.minimum(i + 1, n_tiles - 1),), memory_space=pltpu.SMEM),
                  pl.BlockSpec(memory_space=pl.ANY),
                  pl.BlockSpec((tm, ep), lambda i: (i, 0)),
                  pl.BlockSpec((tm, d), lambda i: (i, 0)),
                  pl.BlockSpec((1, 1, d), lambda i: (row_of_tile(i, tm), 0, 5)),
                  _whole((1, d)), _whole((1, d))],
        out_specs=pl.BlockSpec((tm, d), lambda i: (i, 0)),
        out_shape=jax.ShapeDtypeStruct((t, d), F32),
        scratch_shapes=[pltpu.VMEM((2 * TOP_K * tm * GATHER_PITCH, d // TOK_SUB), F32),
                        pltpu.SemaphoreType.DMA((2,))],
        compiler_params=_cparams(("arbitrary",)),
        name=name,
    )(dest_flat, dest_flat, ys, gates, base, mods3, ln_g.reshape(1, d), ln_b.reshape(1, d))


def _layer_pair(xp, xs, s_f, s_b, c, c_ctx, lp, alpha):
    bp, lp_len, d = xp.shape
    bs, ls_len, _ = xs.shape
    tp, ts = bp * lp_len, bs * ls_len
    assert lp_len % DFT_BLOCK == 0 and ls_len % DFT_BLOCK == 0 and DFT_BLOCK % (ls_len // DFT_BLOCK) == 0
    n_exp = lp['router_w'].shape[1]
    r = MOE_ROWS

    mod_rows = 8 * ((1 + bs + 7) // 8)
    cond = jnp.zeros((mod_rows, d), F32).at[0].set(c_ctx).at[1:1 + bs].set(c)
    mods3 = _ada(cond, lp['w_ada'], lp['b_ada']).reshape(mod_rows, 1, 6 * d)
    row_ctx = lambda i, tile: 0
    row_lat = lambda i, tile: 1 + (i * tile) // ls_len

    w_in = lp['w_in'].astype(BF16)
    ep = max(n_exp, V7X_LANES)
    rw = jnp.zeros((d, ep), F32).at[:, :n_exp].set(lp['router_w'])
    rwh = rw.astype(BF16)
    rwl = (rw - rwh.astype(F32)).astype(BF16)
    rb = jnp.zeros((1, ep), F32).at[0, :n_exp].set(lp['router_b'])
    out_wts = (lp['w_out'].astype(BF16), lp['sh_w_gate'].astype(BF16), lp['sh_w_up'].astype(BF16),
               lp['sh_w_down'].astype(BF16), rwh, rwl, lp['ln1_g'], lp['ln1_b'], rb)
    cw = lp['hy_bias'].shape[1]
    hy_cols = (HY_ORDER + 1) * cw
    lg = jnp.stack([jax.nn.log_sigmoid(lp['ret_decay_f'].astype(F32)),
                    jax.nn.log_sigmoid(lp['ret_decay_b'].astype(F32))])
    dk = s_f.shape[-1]
    zero_state = jnp.zeros((bp, RET_HEADS, dk, dk), F32)

    def mixers(x2, n_seq, L, row_of_tile, s0f, s0b, cnt0, latent, tag):
        proj = _proj(x2, mods3, row_of_tile, w_in, 'proj_' + tag, tm=min(PROJ_ROWS, L if latent else x2.shape[0]))
        y_hy = _hyena(proj, n_seq, L, lp, cw, tag)
        y_ret, nf, nb_ = _retention(proj, n_seq, L, hy_cols, lg, s0f, s0b, latent, 'retention_' + tag)
        routed = _out_stage(y_hy, y_ret, x2, mods3, row_of_tile, out_wts, cnt0, alpha, n_exp, 'out_' + tag)
        return routed, nf, nb_

    (base_p, h2t_p, idx_p, gate_p, rank_p, cnt_p), nf, nb_ = mixers(
        xp.reshape(tp, d), bp, lp_len, row_ctx, zero_state, zero_state, jnp.zeros((1, ep), F32), False, 'ctx')
    (base_s, h2t_s, idx_s, gate_s, rank_s, cnt_all), _, _ = mixers(
        xs.reshape(ts, d), bs, ls_len, row_lat, s_f, s_b, cnt_p, True, 'lat')

    t = tp + ts
    counts = cnt_all[0, :n_exp].astype(jnp.int32)
    padded = (counts + r - 1) // r * r
    pend = jnp.cumsum(padded)
    pstart = pend - padded
    dest_p = (pstart[idx_p[:, :TOP_K]] + rank_p[:, :TOP_K]).reshape(tp * TOP_K)
    dest_s = (pstart[idx_s[:, :TOP_K]] + rank_s[:, :TOP_K]).reshape(ts * TOP_K)
    dest = jnp.concatenate([dest_p, dest_s])
    nb = (t * TOP_K) // r + n_exp
    blk_start = jnp.arange(nb, dtype=jnp.int32) * r
    blk_e = jnp.minimum(jnp.sum(pend[None, :] <= blk_start[:, None], axis=1), n_exp - 1).astype(jnp.int32)
    n_used = (pend[-1:] // r).astype(jnp.int32)
    zpos = jnp.concatenate([pstart + counts, padded - counts, n_used]).astype(jnp.int32)
    xsort = _dispatch(h2t_p, h2t_s, dest, zpos, nb, n_exp)
    ysort = _gmm(xsort, blk_e, n_used, lp['exp_w_gate'], lp['exp_w_up'], lp['exp_w_down'], nb)
    y_p = _combine(ysort, dest_p, gate_p, base_p, mods3, row_ctx, lp['ln2_g'], lp['ln2_b'], 'moe_combine_ctx')
    y_s = _combine(ysort, dest_s, gate_s, base_s, mods3, row_lat, lp['ln2_g'], lp['ln2_b'], 'moe_combine_lat')
    return y_p.reshape(bp, lp_len, d), y_s.reshape(bs, ls_len, d), nf, nb_


def kernel(x_prompt, x_sample, state_ret_fwd, state_ret_bwd, c, c_ctx, w_in, hy_conv_w, hy_conv_b, hy_ffn_w1, hy_ffn_b1, hy_ffn_w2, hy_ffn_b2, hy_ffn_w3, hy_sin_freq, hy_bias, ret_decay_f, ret_decay_b, w_out, w_ada, b_ada, ln1_g, ln1_b, ln2_g, ln2_b, router_w, router_b, exp_w_gate, exp_w_up, exp_w_down, sh_w_gate, sh_w_up, sh_w_down):
    depth = w_in.shape[0]
    alpha = (2.0 * depth) ** 0.25
    params = dict(w_in=w_in, hy_conv_w=hy_conv_w, hy_conv_b=hy_conv_b, hy_ffn_w1=hy_ffn_w1, hy_ffn_b1=hy_ffn_b1,
                  hy_ffn_w2=hy_ffn_w2, hy_ffn_b2=hy_ffn_b2, hy_ffn_w3=hy_ffn_w3, hy_sin_freq=hy_sin_freq,
                  hy_bias=hy_bias, ret_decay_f=ret_decay_f, ret_decay_b=ret_decay_b, w_out=w_out, w_ada=w_ada,
                  b_ada=b_ada, ln1_g=ln1_g, ln1_b=ln1_b, ln2_g=ln2_g, ln2_b=ln2_b, router_w=router_w,
                  router_b=router_b, exp_w_gate=exp_w_gate, exp_w_up=exp_w_up, exp_w_down=exp_w_down,
                  sh_w_gate=sh_w_gate, sh_w_up=sh_w_up, sh_w_down=sh_w_down)
    y_p, y_s = x_prompt, x_sample
    new_f, new_b = [], []
    for l in range(depth):
        lp = {k: v[l] for k, v in params.items()}
        y_p, y_s, s_f, s_b = _layer_pair(y_p, y_s, state_ret_fwd[:, l], state_ret_bwd[:, l], c, c_ctx, lp, alpha)
        new_f.append(s_f.astype(x_prompt.dtype))
        new_b.append(s_b.astype(x_prompt.dtype))
    return (y_p, y_s, jnp.stack(new_f, axis=1), jnp.stack(new_b, axis=1))
```

```python
import functools
import math

import numpy as np
import jax
import jax.numpy as jnp
from jax import lax
from jax.experimental import pallas as pl
from jax.experimental.pallas import tpu as pltpu

F32 = jnp.float32
BF16 = jnp.bfloat16

RET_HEADS = 8
TOP_K = 8
GRID_W = 64
HY_ORDER = 2
HY_BANDS = 16
HY_TARGET = 1e-2
HY_FAST = 0.3
HY_SLOW = 1.5
ROUTED_SCALE = 2.5
ROPE_BASE = 10000.0
LN_EPS = 1e-5
GN_EPS = 1e-6

V7X_LANES = 128
V7X_VMEM_LIMIT_BYTES = 56 * 1024 * 1024

DFT_BLOCK = 256
RET_CHUNK = 256
PROJ_ROWS = 1024
MOE_ROWS = 512
TOK_SUB = 8


def _cparams(sem, vmem=V7X_VMEM_LIMIT_BYTES):
    return pltpu.CompilerParams(dimension_semantics=sem, vmem_limit_bytes=vmem)


def _dot(a, b):
    return jnp.dot(a, b, preferred_element_type=F32)


def _split_bf16(a):
    hi = a.astype(BF16)
    lo = (a - hi.astype(F32)).astype(BF16)
    return hi, lo


def _dot3(a, b):
    ah, al = _split_bf16(a)
    bh, bl = _split_bf16(b)
    return _dot(ah, bh) + _dot(ah, bl) + _dot(al, bh)


def _silu(x):
    return x * jax.nn.sigmoid(x)


def _whole(shape):
    n = len(shape)
    return pl.BlockSpec(shape, lambda *_: (0,) * n)


def _ada_kernel(c_ref, w_ref, b_ref, o_ref):
    o_ref[...] = _dot3(_silu(c_ref[...]), w_ref[...]) + b_ref[...]


def _ada(cond, w_ada, b_ada, tn=1024):
    rows, d = cond.shape
    n = w_ada.shape[1]
    return pl.pallas_call(
        _ada_kernel,
        grid=(n // tn,),
        in_specs=[pl.BlockSpec((rows, d), lambda j: (0, 0)),
                  pl.BlockSpec((d, tn), lambda j: (0, j)),
                  pl.BlockSpec((1, tn), lambda j: (0, j))],
        out_specs=pl.BlockSpec((rows, tn), lambda j: (0, j)),
        out_shape=jax.ShapeDtypeStruct((rows, n), F32),
        compiler_params=_cparams(("parallel",)),
        name='ada_mods',
    )(cond, w_ada, b_ada.reshape(1, n))


def _proj_kernel(x_ref, sc_ref, sh_ref, w_ref, o_ref, h_scr):
    @pl.when(pl.program_id(1) == 0)
    def _():
        h_scr[...] = (x_ref[...] * (1.0 + sc_ref[0]) + sh_ref[0]).astype(BF16)

    o_ref[...] = _dot(h_scr[...], w_ref[...]).astype(o_ref.dtype)


def _proj(x, mods3, row_of_tile, w_in_bf16, name, tm, tn=1024):
    t, d = x.shape
    n = w_in_bf16.shape[1]
    row_of_tile = functools.partial(row_of_tile, tile=tm)
    return pl.pallas_call(
        _proj_kernel,
        name=name,
        grid=(t // tm, n // tn),
        in_specs=[pl.BlockSpec((tm, d), lambda i, j: (i, 0)),
                  pl.BlockSpec((1, 1, d), lambda i, j: (row_of_tile(i), 0, 1)),
                  pl.BlockSpec((1, 1, d), lambda i, j: (row_of_tile(i), 0, 0)),
                  pl.BlockSpec((d, tn), lambda i, j: (0, j))],
        out_specs=pl.BlockSpec((tm, tn), lambda i, j: (i, j)),
        out_shape=jax.ShapeDtypeStruct((t, n), BF16),
        scratch_shapes=[pltpu.VMEM((tm, d), BF16)],
        compiler_params=_cparams(("parallel", "arbitrary")),
    )(x, mods3, mods3, w_in_bf16)


@functools.lru_cache(maxsize=None)
def _hy_mats(L):
    nn2 = L // DFT_BLOCK
    n_full = 2 * L
    n2_full = 2 * nn2
    g_rows = DFT_BLOCK // nn2
    k2 = np.arange(nn2)
    n2 = np.arange(nn2)
    s1 = np.zeros((nn2, DFT_BLOCK, DFT_BLOCK), np.complex128)
    for g in range(nn2):
        for s in range(g_rows):
            n1 = g * g_rows + s
            blk = (np.exp(-2j * np.pi * np.outer(k2 + 0.5, n2) / n2_full)
                   * np.exp(-2j * np.pi * (k2 + 0.5) * n1 / n_full)[:, None])
            s1[g][np.ix_(k2 * g_rows + s, n2 * g_rows + s)] = blk
    m1f = np.concatenate([s1.real, s1.imag], axis=1)
    a = np.conj(s1).transpose(0, 2, 1)
    m1i = (2.0 / n_full) * np.concatenate([a.real, -a.imag], axis=2)
    k1 = np.arange(DFT_BLOCK)
    f = np.exp(-2j * np.pi * np.outer(k1, k1) / DFT_BLOCK)
    f2f = np.block([[f.real, -f.imag], [f.imag, f.real]])
    f2i = np.block([[f.real, f.imag], [-f.imag, f.real]])
    return tuple(np.asarray(m, np.float32) for m in (m1f, m1i, f2f, f2i))


def _hy_unroll(nn2):
    return 4 if nn2 % 4 == 0 else 1


def _hy_fwd_stage1(u_scr, m1f_ref, v_scr, nn2):
    g_rows = DFT_BLOCK // nn2

    def body(g, carry):
        off = pl.multiple_of(g * g_rows, g_rows)
        parts = [u_scr[pl.ds(n2 * DFT_BLOCK + off, g_rows), :] for n2 in range(nn2)]
        inp = jnp.concatenate(parts, axis=0).astype(BF16)
        out = _dot(m1f_ref[g], inp)
        for ri in range(2):
            for k2 in range(nn2):
                r0 = (ri * nn2 + k2) * g_rows
                v_scr[k2, pl.ds(ri * DFT_BLOCK + off, g_rows), :] = out[r0:r0 + g_rows].astype(BF16)
        return carry

    lax.fori_loop(0, nn2, body, 0, unroll=2 if nn2 % 2 == 0 else 1)


def _hy_inv_stage1(w_scr, m1i_ref, y_scr, nn2):
    g_rows = DFT_BLOCK // nn2

    def body(g, carry):
        off = pl.multiple_of(g * g_rows, g_rows)
        parts = [w_scr[k2, pl.ds(ri * DFT_BLOCK + off, g_rows), :]
                 for ri in range(2) for k2 in range(nn2)]
        inp = jnp.concatenate(parts, axis=0)
        out = _dot(m1i_ref[g], inp)
        for n2 in range(nn2):
            y_scr[pl.ds(n2 * DFT_BLOCK + off, g_rows), :] = out[n2 * g_rows:(n2 + 1) * g_rows]
        return carry

    lax.fori_loop(0, nn2, body, 0, unroll=2 if nn2 % 2 == 0 else 1)


def _short_conv(u, w, b):
    L = u.shape[0]
    row = lax.broadcasted_iota(jnp.int32, u.shape, 0)
    prev = jnp.where(row == 0, 0.0, pltpu.roll(u, 1, 0))
    nxt = jnp.where(row == L - 1, 0.0, pltpu.roll(u, L - 1, 0))
    return prev * w[0:1] + u * w[1:2] + nxt * w[2:3] + b


def _hy_filter_kernel(z_ref, w1_ref, b1_ref, w2_ref, b2_ref, w3_ref, fr_ref, dl_ref, o_ref, *, n_half_tiles, c):
    i = pl.program_id(0)
    z = z_ref[...]
    fr = fr_ref[...]
    h = jnp.sin(fr * (_dot3(z, w1_ref[...]) + b1_ref[...]))
    h = jnp.sin(fr * (_dot3(h, w2_ref[...]) + b2_ref[...]))
    h = _dot3(h, w3_ref[...])
    tn = z[:, 0:1]
    decay = jnp.exp(-tn * jnp.abs(dl_ref[...]))
    valid = z[:, 2 * HY_BANDS + 1:2 * HY_BANDS + 2]
    second = i >= n_half_tiles
    for o in range(HY_ORDER):
        fwd = h[:, (2 * o) * c:(2 * o + 1) * c]
        bwd = h[:, (2 * o + 1) * c:(2 * o + 2) * c]
        val = jnp.where(second, -bwd, fwd) * decay * valid
        o_ref[:, o * c:(o + 1) * c] = val


def _hy_filters(L, w1, b1, w2, b2, w3, freq, name, rt=256):
    c = w3.shape[1] // (2 * HY_ORDER)
    fh = w1.shape[1]
    r = np.arange(2 * L)
    t = np.where(r < L, r, 2 * L - r).astype(np.float64)
    f = np.linspace(1e-4, HY_BANDS - 1, HY_BANDS)
    ang = (2.0 * math.pi / L) * t[:, None] * f[None, :]
    z = np.zeros((2 * L, V7X_LANES), np.float32)
    z[:, 0] = t / (L - 1)
    z[:, 1:1 + HY_BANDS] = np.cos(ang)
    z[:, 1 + HY_BANDS:1 + 2 * HY_BANDS] = -np.sin(ang)
    z[:, 1 + 2 * HY_BANDS] = (r != L)
    w1p = jnp.zeros((V7X_LANES, fh), F32).at[:w1.shape[0]].set(w1)
    deltas = np.linspace(math.log(HY_TARGET) / HY_SLOW, math.log(HY_TARGET) / HY_FAST, c).astype(np.float32)
    kern = functools.partial(_hy_filter_kernel, n_half_tiles=L // rt, c=c)
    return pl.pallas_call(
        kern,
        grid=(2 * L // rt,),
        in_specs=[pl.BlockSpec((rt, V7X_LANES), lambda i: (i, 0)),
                  _whole((V7X_LANES, fh)), _whole((1, fh)), _whole((fh, fh)), _whole((1, fh)),
                  _whole(w3.shape), _whole((1, fh)), _whole((1, c))],
        out_specs=pl.BlockSpec((rt, HY_ORDER * c), lambda i: (i, 0)),
        out_shape=jax.ShapeDtypeStruct((2 * L, HY_ORDER * c), F32),
        compiler_params=_cparams(("parallel",)),
        name=name,
    )(jnp.asarray(z), w1p, b1.reshape(1, fh), w2, b2.reshape(1, fh), w3, freq.reshape(1, fh),
      jnp.asarray(deltas).reshape(1, c))


def _hy_spec_kernel(f_ref, m1f_ref, f2f_ref, o_ref, u_scr, v_scr, acc_scr, *, nn2):
    half = pl.program_id(1)
    u_scr[...] = f_ref[...]
    _hy_fwd_stage1(u_scr, m1f_ref, v_scr, nn2)

    def body(k2, carry):
        z = _dot(f2f_ref[...], v_scr[k2])

        @pl.when(half == 0)
        def _():
            acc_scr[k2] = z

        @pl.when(half == 1)
        def _():
            sign = 1.0 - 2.0 * (k2 % 2)
            a = acc_scr[k2]
            o_ref[0, k2] = (a[:DFT_BLOCK] + sign * z[DFT_BLOCK:]).astype(o_ref.dtype)
            o_ref[1, k2] = (a[DFT_BLOCK:] - sign * z[:DFT_BLOCK]).astype(o_ref.dtype)

        return carry

    lax.fori_loop(0, nn2, body, 0, unroll=2 if nn2 % 2 == 0 else 1)


def _hy_spectrum(filt, L, name, ct=256):
    nn2 = L // DFT_BLOCK
    cols = filt.shape[1]
    m1f, _, f2f, _ = _hy_mats(L)
    kern = functools.partial(_hy_spec_kernel, nn2=nn2)
    return pl.pallas_call(
        kern,
        grid=(cols // ct, 2),
        in_specs=[pl.BlockSpec((L, ct), lambda j, h: (h, j)),
                  pl.BlockSpec(memory_space=pltpu.VMEM),
                  pl.BlockSpec(memory_space=pltpu.VMEM)],
        out_specs=pl.BlockSpec((2, nn2, DFT_BLOCK, ct), lambda j, h: (0, 0, 0, j)),
        out_shape=jax.ShapeDtypeStruct((2, nn2, DFT_BLOCK, cols), BF16),
        scratch_shapes=[pltpu.VMEM((L, ct), F32),
                        pltpu.VMEM((nn2, 2 * DFT_BLOCK, ct), BF16),
                        pltpu.VMEM((nn2, 2 * DFT_BLOCK, ct), F32)],
        compiler_params=_cparams(("parallel", "arbitrary")),
        name=name,
    )(filt, jnp.asarray(m1f, BF16), jnp.asarray(f2f, BF16))


def _hy_conv_kernel(u_ref, g_ref, cwu_ref, cbu_ref, cwg_ref, cbg_ref, bias_ref, ks_ref,
                    m1f_ref, m1i_ref, f2f_ref, f2i_ref, o_ref, u_scr, v_scr, w_scr, y_scr, *, nn2, conv_u):
    u = u_ref[...].astype(F32)
    if conv_u:
        u = _short_conv(u, cwu_ref[...], cbu_ref[...])
    u_scr[...] = u
    _hy_fwd_stage1(u_scr, m1f_ref, v_scr, nn2)

    def body(k2, carry):
        z = _dot(f2f_ref[...], v_scr[k2])
        zr, zi = z[:DFT_BLOCK], z[DFT_BLOCK:]
        kr = ks_ref[0, k2].astype(F32)
        ki = ks_ref[1, k2].astype(F32)
        y = jnp.concatenate([zr * kr - zi * ki, zr * ki + zi * kr], axis=0).astype(BF16)
        w_scr[k2] = _dot(f2i_ref[...], y).astype(BF16)
        return carry

    lax.fori_loop(0, nn2, body, 0, unroll=_hy_unroll(nn2))
    _hy_inv_stage1(w_scr, m1i_ref, y_scr, nn2)
    gate = _short_conv(g_ref[...].astype(F32), cwg_ref[...], cbg_ref[...])
    o_ref[...] = (gate * (y_scr[...] + bias_ref[...] * u_scr[...])).astype(o_ref.dtype)


def _hy_conv(u_arr, u_col0, conv_u, proj, g_col0, n_seq, L, conv_w, conv_b, bias_o, spec, spec_col0, c, name, ct=256):
    nn2 = L // DFT_BLOCK
    m1f, m1i, f2f, f2i = (jnp.asarray(m, BF16) for m in _hy_mats(L))
    nct = c // ct
    ub, gb, sb = u_col0 // ct, g_col0 // ct, spec_col0 // ct
    kern = functools.partial(_hy_conv_kernel, nn2=nn2, conv_u=conv_u)
    vm = pl.BlockSpec(memory_space=pltpu.VMEM)
    return pl.pallas_call(
        kern,
        grid=(nct, n_seq),
        in_specs=[pl.BlockSpec((L, ct), lambda j, b: (b, ub + j)),
                  pl.BlockSpec((L, ct), lambda j, b: (b, gb + j)),
                  pl.BlockSpec((3, ct), lambda j, b: (0, ub + j)),
                  pl.BlockSpec((1, ct), lambda j, b: (0, ub + j)),
                  pl.BlockSpec((3, ct), lambda j, b: (0, gb + j)),
                  pl.BlockSpec((1, ct), lambda j, b: (0, gb + j)),
                  pl.BlockSpec((1, ct), lambda j, b: (0, j)),
                  pl.BlockSpec((2, nn2, DFT_BLOCK, ct), lambda j, b: (0, 0, 0, sb + j)),
                  vm, vm, vm, vm],
        out_specs=pl.BlockSpec((L, ct), lambda j, b: (b, j)),
        out_shape=jax.ShapeDtypeStruct((n_seq * L, c), BF16),
        scratch_shapes=[pltpu.VMEM((L, ct), F32),
                        pltpu.VMEM((nn2, 2 * DFT_BLOCK, ct), BF16),
                        pltpu.VMEM((nn2, 2 * DFT_BLOCK, ct), BF16),
                        pltpu.VMEM((L, ct), F32)],
        compiler_params=_cparams(("parallel", "parallel")),
        name=name,
    )(u_arr, proj, conv_w, conv_b, conv_w, conv_b, bias_o, spec, m1f, m1i, f2f, f2i)


def _hyena(proj, n_seq, L, lp, c, tag):
    filt = _hy_filters(L, lp['hy_ffn_w1'], lp['hy_ffn_b1'], lp['hy_ffn_w2'], lp['hy_ffn_b2'],
                       lp['hy_ffn_w3'], lp['hy_sin_freq'], 'hy_filter_' + tag)
    spec = _hy_spectrum(filt, L, 'hy_spectrum_' + tag)
    cw = lp['hy_conv_w']
    cb = lp['hy_conv_b'].reshape(1, -1)
    bias = lp['hy_bias']
    z = _hy_conv(proj, 0, True, proj, c, n_seq, L, cw, cb, bias[0:1], spec, 0, c, 'hy_conv1_' + tag)
    z = _hy_conv(z, 0, False, proj, 2 * c, n_seq, L, cw, cb, bias[1:2], spec, c, c, 'hy_conv2_' + tag)
    return z


def _rope_tables(L, dk):
    rows = L // GRID_W
    row = np.repeat(np.arange(rows), GRID_W).astype(np.float64)
    col = np.tile(np.arange(GRID_W), rows).astype(np.float64)
    quarter = dk // 4
    inv = ROPE_BASE ** (-np.arange(quarter, dtype=np.float64) / quarter)
    ar = row[:, None] * inv[None, :]
    ac = col[:, None] * inv[None, :]
    cos = np.concatenate([np.cos(ar), np.cos(ar), np.cos(ac), np.cos(ac)], axis=1)
    sin = np.concatenate([-np.sin(ar), np.sin(ar), -np.sin(ac), np.sin(ac)], axis=1)
    return jnp.asarray(cos, F32), jnp.asarray(sin, F32)


def _rope(x, cos, sin_signed):
    dk = x.shape[1]
    q = dk // 4
    lane = lax.broadcasted_iota(jnp.int32, x.shape, 1)
    first = (lane % (2 * q)) < q
    swapped = jnp.where(first, pltpu.roll(x, dk - q, 1), pltpu.roll(x, q, 1))
    return x * cos + swapped * sin_signed


def _ret_kernel(lg_ref, q_ref, k_ref, v_ref, g_ref, cos_ref, sin_ref, s0f_ref, s0b_ref,
                y_ref, sf_ref, sb_ref, qd_scr, u_scr, s_scr, o_scr, *, latent, chunk):
    h = pl.program_id(1)
    L, dk = q_ref.shape
    n = L // chunk
    lgf = lg_ref[0, h]
    lgb = lg_ref[1, h]
    pos = lax.broadcasted_iota(jnp.int32, (chunk, dk), 0).astype(F32)
    qd_f = jnp.exp(lgf * (pos + 1.0))
    qd_b = jnp.exp(lgb * (chunk - pos))
    kd_f = jnp.exp(lgf * (chunk - 1.0 - pos))
    kd_b = jnp.exp(lgb * pos)
    pos_r = lax.broadcasted_iota(jnp.int32, (chunk, chunk), 0).astype(F32)
    pos_c = lax.broadcasted_iota(jnp.int32, (chunk, chunk), 1).astype(F32)
    diff = pos_r - pos_c
    mask = (jnp.where(diff >= 0, jnp.exp(lgf * jnp.maximum(diff, 0.0)), 0.0)
            + jnp.where(diff <= 0, jnp.exp(lgb * jnp.maximum(-diff, 0.0)), 0.0))
    unroll = 2 if n % 2 == 0 else 1

    def intra(i, carry):
        r0 = pl.multiple_of(i * chunk, chunk)
        rows = pl.ds(r0, chunk)
        q = q_ref[rows, :].astype(F32)
        k = k_ref[rows, :].astype(F32) * (dk ** -0.5)
        if latent:
            q = _rope(q, cos_ref[rows, :], sin_ref[rows, :])
            k = _rope(k, cos_ref[rows, :], sin_ref[rows, :])
        qd_scr[rows, :dk] = (q * qd_f).astype(BF16)
        qd_scr[rows, dk:] = (q * qd_b).astype(BF16)
        vi = v_ref[rows, :]
        att = lax.dot_general(q.astype(BF16), k.astype(BF16), (((1,), (1,)), ((), ())),
                              preferred_element_type=F32) * mask
        o_scr[rows, :] = _dot(att.astype(BF16), vi)
        kd = jnp.concatenate([k * kd_f, k * kd_b], axis=1)
        u_scr[i] = _dot(kd.T.astype(BF16), vi)
        return carry

    lax.fori_loop(0, n, intra, 0, unroll=unroll)

    c_f = jnp.exp(lgf * chunk)
    c_b = jnp.exp(lgb * chunk)

    def fwd_state(i, s):
        s_scr[i, :dk, :] = s.astype(BF16)
        return c_f * s + u_scr[i, :dk, :]

    def bwd_state(j, s):
        i = n - 1 - j
        s_scr[i, dk:, :] = s.astype(BF16)
        return c_b * s + u_scr[i, dk:, :]

    sf_ref[0, 0] = lax.fori_loop(0, n, fwd_state, s0f_ref[0, 0])
    sb_ref[0, 0] = lax.fori_loop(0, n, bwd_state, s0b_ref[0, 0])

    def inter(i, carry):
        rows = pl.ds(pl.multiple_of(i * chunk, chunk), chunk)
        o = o_scr[rows, :] + _dot(qd_scr[rows, :], s_scr[i])
        mu = jnp.mean(o, axis=-1, keepdims=True)
        var = jnp.mean(jnp.square(o - mu), axis=-1, keepdims=True)
        o = (o - mu) * lax.rsqrt(var + GN_EPS)
        y_ref[rows, :] = (o * _silu(g_ref[rows, :].astype(F32))).astype(y_ref.dtype)
        return carry

    lax.fori_loop(0, n, inter, 0, unroll=unroll)


def _retention(proj, n_seq, L, col0, lg, s0_f, s0_b, latent, name):
    dk = s0_f.shape[-1]
    hb = col0 // dk
    chunk = min(RET_CHUNK, L)
    cos, sin = _rope_tables(L, dk) if latent else (jnp.zeros((8, dk), F32), jnp.zeros((8, dk), F32))
    tbl = pl.BlockSpec(cos.shape, lambda b, h: (0, 0))
    seq = lambda part: pl.BlockSpec((L, dk), lambda b, h: (b, hb + part * RET_HEADS + h))
    st = pl.BlockSpec((1, 1, dk, dk), lambda b, h: (b, h, 0, 0))
    kern = functools.partial(_ret_kernel, latent=latent, chunk=chunk)
    return pl.pallas_call(
        kern,
        grid=(n_seq, RET_HEADS),
        in_specs=[pl.BlockSpec(memory_space=pltpu.SMEM),
                  seq(0), seq(1), seq(2), seq(3), tbl, tbl, st, st],
        out_specs=[pl.BlockSpec((L, dk), lambda b, h: (b, h)), st, st],
        out_shape=[jax.ShapeDtypeStruct((n_seq * L, RET_HEADS * dk), BF16),
                   jax.ShapeDtypeStruct((n_seq, RET_HEADS, dk, dk), F32),
                   jax.ShapeDtypeStruct((n_seq, RET_HEADS, dk, dk), F32)],
        scratch_shapes=[pltpu.VMEM((L, 2 * dk), BF16),
                        pltpu.VMEM((L // chunk, 2 * dk, dk), F32), pltpu.VMEM((L // chunk, 2 * dk, dk), BF16),
                        pltpu.VMEM((L, dk), F32)],
        compiler_params=_cparams(("parallel", "parallel")),
        name=name,
    )(lg, proj, proj, proj, proj, cos, sin, s0_f, s0_b)


def _layernorm(r, g, b):
    mu = jnp.mean(r, axis=-1, keepdims=True)
    var = jnp.mean(jnp.square(r - mu), axis=-1, keepdims=True)
    return (r - mu) * lax.rsqrt(var + LN_EPS) * g + b


def _route(logits, router_b, tri, before0, n_exp):
    tm, ep = logits.shape
    lane = lax.broadcasted_iota(jnp.int32, (tm, ep), 1).astype(F32)
    scores = jax.nn.sigmoid(logits)
    work = jnp.where(lane < n_exp, scores + router_b, -jnp.inf)
    sel = jnp.zeros((tm, ep), jnp.bool_)
    idx_cols = []
    s_cols = []
    for _ in range(TOP_K):
        m = jnp.max(work, axis=-1, keepdims=True)
        first = jnp.min(jnp.where(work == m, lane, float(ep)), axis=-1, keepdims=True)
        hit = lane == first
        idx_cols.append(first)
        s_cols.append(jnp.sum(jnp.where(hit, scores, 0.0), axis=-1, keepdims=True))
        sel = jnp.logical_or(sel, hit)
        work = jnp.where(hit, -jnp.inf, work)
    denom = functools.reduce(lambda a, b: a + b, s_cols)
    before = _dot(tri, sel.astype(BF16)) + before0
    idx_out = jnp.zeros((tm, ep), F32)
    gate_out = jnp.zeros((tm, ep), F32)
    rank_out = jnp.zeros((tm, ep), jnp.int32)
    for j in range(TOP_K):
        hit = lane == idx_cols[j]
        rank_j = jnp.sum(jnp.where(hit, before, 0.0), axis=-1, keepdims=True).astype(jnp.int32)
        idx_out = jnp.where(lane == j, idx_cols[j], idx_out)
        gate_out = jnp.where(lane == j, ROUTED_SCALE * s_cols[j] / denom, gate_out)
        rank_out = jnp.where(lane == j, rank_j, rank_out)
    counts = before0 + jnp.sum(sel.astype(F32), axis=0, keepdims=True)
    return idx_out.astype(jnp.int32), gate_out, rank_out, counts


def _out_kernel(yh_ref, yr_ref, x_ref, g1_ref, sc2_ref, sh2_ref, g2_ref, lng_ref, lnb_ref,
                wo_ref, wg_ref, wu_ref, wd_ref, rwh_ref, rwl_ref, rb_ref, tri_ref, cnt0_ref,
                base_ref, h2_ref, idx_ref, gate_ref, rank_ref, cnt_ref, *, alpha, n_exp):
    @pl.when(pl.program_id(0) == 0)
    def _():
        cnt_ref[...] = cnt0_ref[...]

    c = yh_ref.shape[1]
    mix = _dot(yh_ref[...], wo_ref[:c]) + _dot(yr_ref[...], wo_ref[c:])
    x1 = _layernorm(alpha * x_ref[...] + g1_ref[0] * mix, lng_ref[...], lnb_ref[...])
    h2 = x1 * (1.0 + sc2_ref[0]) + sh2_ref[0]
    _to_token_tiles(h2_ref, h2)
    hh, hl = _split_bf16(h2)
    logits = _dot(hh, rwh_ref[...]) + _dot(hh, rwl_ref[...]) + _dot(hl, rwh_ref[...])
    a = (_silu(_dot(hh, wg_ref[...])) * _dot(hh, wu_ref[...])).astype(BF16)
    base_ref[...] = alpha * x1 + g2_ref[0] * _dot(a, wd_ref[...])
    idx_ref[...], gate_ref[...], rank_ref[...], cnt_ref[...] = _route(
        logits, rb_ref[...], tri_ref[...], cnt_ref[...], n_exp)


def _out_stage(y_hy, y_ret, x, mods3, row_of_tile, wts, cnt0, alpha, n_exp, name, tm=256):
    t, d = x.shape
    c = y_hy.shape[1]
    wo, sg, su, sd, rwh, rwl, ln_g, ln_b, rb = wts
    ep = rwh.shape[1]
    tri = jnp.asarray(np.tril(np.ones((tm, tm), np.float32), -1), BF16)
    row_of_tile = functools.partial(row_of_tile, tile=tm)
    mod = lambda k: pl.BlockSpec((1, 1, d), lambda i: (row_of_tile(i), 0, k))
    vm = pl.BlockSpec(memory_space=pltpu.VMEM)
    tile = lambda w: pl.BlockSpec((tm, w), lambda i: (i, 0))
    kern = functools.partial(_out_kernel, alpha=alpha, n_exp=n_exp)
    return pl.pallas_call(
        kern,
        grid=(t // tm,),
        in_specs=[tile(c), tile(c), tile(d), mod(2), mod(4), mod(3), mod(5),
                  _whole((1, d)), _whole((1, d)), vm, vm, vm, vm, vm, vm, vm, vm, vm],
        out_specs=[tile(d), pl.BlockSpec((tm * TOK_SUB, V7X_LANES), lambda i: (i, 0)),
                   tile(ep), tile(ep), tile(ep), _whole((1, ep))],
        out_shape=[jax.ShapeDtypeStruct((t, d), F32), jax.ShapeDtypeStruct((t * TOK_SUB, V7X_LANES), jnp.uint32),
                   jax.ShapeDtypeStruct((t, ep), jnp.int32), jax.ShapeDtypeStruct((t, ep), F32),
                   jax.ShapeDtypeStruct((t, ep), jnp.int32), jax.ShapeDtypeStruct((1, ep), F32)],
        compiler_params=_cparams(("arbitrary",)),
        name=name,
    )(y_hy, y_ret, x, mods3, mods3, mods3, mods3, ln_g.reshape(1, d), ln_b.reshape(1, d),
      wo, sg, su, sd, rwh, rwl, rb, tri, cnt0)


_HI_MASK = np.uint32(0xFFFF0000)


def _bf16_bits(v):
    return lax.bitcast_convert_type(v.astype(BF16).astype(F32), jnp.uint32)


def _to_token_tiles(ref, val):
    rows, d = val.shape
    half = d // 2
    for s in range(TOK_SUB):
        lo = _bf16_bits(val[:, s * V7X_LANES:(s + 1) * V7X_LANES]) >> 16
        hi = _bf16_bits(val[:, half + s * V7X_LANES:half + (s + 1) * V7X_LANES]) & _HI_MASK
        ref[pl.ds(s, rows, stride=TOK_SUB), :] = lo | hi


def _unpack_tile_rows(words):
    return (lax.bitcast_convert_type(words << 16, F32), lax.bitcast_convert_type(words & _HI_MASK, F32))


def _from_token_tiles(ref, row0, rows):
    parts = [_unpack_tile_rows(ref[pl.ds(row0 * TOK_SUB + s, rows, stride=TOK_SUB), :]) for s in range(TOK_SUB)]
    return jnp.concatenate([p[0] for p in parts] + [p[1] for p in parts], axis=1)


def _row_copy(src_ref, src_row, dst_ref, dst_row, sem):
    src = src_ref.at[pl.ds(pl.multiple_of(src_row * TOK_SUB, TOK_SUB), TOK_SUB), :]
    dst = dst_ref.at[pl.ds(pl.multiple_of(dst_row * TOK_SUB, TOK_SUB), TOK_SUB), :]
    return pltpu.make_async_copy(src, dst, sem)


def _dispatch_kernel(dest_ref, zpos_ref, hc_ref, hl_ref, xs_ref, zero_scr, sem, *, n_exp, tm, ctx_tiles, nb):
    i = pl.program_id(0)

    @pl.when(i == 0)
    def _():
        zero_scr[...] = jnp.zeros_like(zero_scr)
        max_piece = zero_scr.shape[0] // TOK_SUB
        pieces = [max_piece >> k for k in range(max_piece.bit_length())]

        def zero_copy(slot, n_tok):
            dst = xs_ref.at[pl.ds(pl.multiple_of(slot * TOK_SUB, TOK_SUB), n_tok * TOK_SUB), :]
            return pltpu.make_async_copy(zero_scr.at[pl.ds(0, n_tok * TOK_SUB), :], dst, sem)

        def each_pad_piece(e, fn):
            pad = zpos_ref[n_exp + e]
            for b in pieces:
                @pl.when((pad & b) != 0)
                def _():
                    fn(zero_copy(zpos_ref[e] + (pad & (-2 * b)), b))

        def each_tail_piece(b, fn):
            for half in range(MOE_ROWS // max_piece):
                fn(zero_copy(b * MOE_ROWS + half * max_piece, max_piece))

        def run(fn):
            lax.fori_loop(0, n_exp, lambda e, c: (each_pad_piece(e, fn), c)[1], 0)
            lax.fori_loop(zpos_ref[2 * n_exp], nb, lambda b, c: (each_tail_piece(b, fn), c)[1], 0)

        run(lambda cp: cp.start())
        run(lambda cp: cp.wait())

    def scatter(h_ref):
        def start(t, c):
            for j in range(TOP_K):
                _row_copy(h_ref, t, xs_ref, dest_ref[t * TOP_K + j], sem).start(priority=j % 2)
            return c

        lax.fori_loop(0, tm, start, 0)
        for _ in range(TOP_K):
            pltpu.make_async_copy(h_ref, xs_ref.at[pl.ds(0, tm * TOK_SUB), :], sem).wait()

    @pl.when(i < ctx_tiles)
    def _():
        scatter(hc_ref)

    @pl.when(i >= ctx_tiles)
    def _():
        scatter(hl_ref)


def _dispatch(h2t_ctx, h2t_lat, dest_flat, zpos, nb, n_exp, tm=256):
    w = h2t_ctx.shape[1]
    ctx_tiles = h2t_ctx.shape[0] // TOK_SUB // tm
    lat_tiles = h2t_lat.shape[0] // TOK_SUB // tm
    kern = functools.partial(_dispatch_kernel, n_exp=n_exp, tm=tm, ctx_tiles=ctx_tiles, nb=nb)
    return pl.pallas_call(
        kern,
        grid=(ctx_tiles + lat_tiles,),
        in_specs=[pl.BlockSpec((tm * TOP_K,), lambda i: (i,), memory_space=pltpu.SMEM),
                  pl.BlockSpec(memory_space=pltpu.SMEM),
                  pl.BlockSpec((tm * TOK_SUB, w), lambda i: (jnp.minimum(i, ctx_tiles - 1), 0)),
                  pl.BlockSpec((tm * TOK_SUB, w), lambda i: (jnp.maximum(i - ctx_tiles, 0), 0))],
        out_specs=pl.BlockSpec(memory_space=pl.ANY),
        out_shape=jax.ShapeDtypeStruct((nb * MOE_ROWS * TOK_SUB, w), jnp.uint32),
        scratch_shapes=[pltpu.VMEM((MOE_ROWS // 2 * TOK_SUB, w), jnp.uint32), pltpu.SemaphoreType.DMA(())],
        compiler_params=_cparams(("arbitrary",)),
        name='moe_dispatch',
    )(dest_flat, zpos, h2t_ctx, h2t_lat)


def _gmm_kernel(be_ref, nu_ref, x_ref, wg_ref, wu_ref, wd_ref, o_ref, wg_scr, wu_scr, wd_scr, y_scr):
    b = pl.program_id(0)
    rows = x_ref.shape[0] // TOK_SUB
    n_used = nu_ref[0]

    @pl.when(b == 0)
    def _():
        y_scr[...] = jnp.zeros_like(y_scr)

    @pl.when(b < n_used)
    def _():
        @pl.when(jnp.logical_or(b == 0, be_ref[b] != be_ref[jnp.maximum(b - 1, 0)]))
        def _():
            wg_scr[...] = wg_ref[0].astype(BF16)
            wu_scr[...] = wu_ref[0].astype(BF16)
            wd_scr[...] = wd_ref[0].astype(BF16)

        _to_token_tiles(o_ref, y_scr[...])
        x = _from_token_tiles(x_ref, 0, rows).astype(BF16)
        a = (_silu(_dot(x, wg_scr[...])) * _dot(x, wu_scr[...])).astype(BF16)
        y_scr[...] = _dot(a, wd_scr[...])

    @pl.when(b == n_used)
    def _():
        _to_token_tiles(o_ref, y_scr[...])

    @pl.when(b > n_used)
    def _():
        o_ref[...] = jnp.zeros_like(o_ref)


def _gmm(xs, blk_e, n_used, wg, wu, wd, nb):
    w = xs.shape[1]
    d, f = wg.shape[1], wg.shape[2]
    r8 = MOE_ROWS * TOK_SUB
    last = lambda b, nu: jnp.minimum(b, nu[0] - 1)
    grid_spec = pltpu.PrefetchScalarGridSpec(
        num_scalar_prefetch=2,
        grid=(nb + 1,),
        in_specs=[pl.BlockSpec((r8, w), lambda b, be, nu: (last(b, nu), 0)),
                  pl.BlockSpec((1, d, f), lambda b, be, nu: (be[last(b, nu)], 0, 0)),
                  pl.BlockSpec((1, d, f), lambda b, be, nu: (be[last(b, nu)], 0, 0)),
                  pl.BlockSpec((1, f, d), lambda b, be, nu: (be[last(b, nu)], 0, 0))],
        out_specs=pl.BlockSpec((r8, w), lambda b, be, nu: (jnp.maximum(b - 1, 0), 0)),
        scratch_shapes=[pltpu.VMEM((d, f), BF16), pltpu.VMEM((d, f), BF16), pltpu.VMEM((f, d), BF16),
                        pltpu.VMEM((MOE_ROWS, d), F32)],
    )
    return pl.pallas_call(
        _gmm_kernel,
        grid_spec=grid_spec,
        out_shape=jax.ShapeDtypeStruct((nb * r8, w), jnp.uint32),
        compiler_params=_cparams(("arbitrary",)),
        name='moe_experts',
    )(blk_e, n_used, xs, wg, wu, wd)


def _combine_kernel(dest_ref, destn_ref, ys_ref, gate_ref, base_ref, g2_ref, lng_ref, lnb_ref, o_ref,
                    rows_scr, sem, *, tm, n_tiles):
    i = pl.program_id(0)
    slot_tok = TOP_K * tm
    slot_rows = slot_tok * TOK_SUB
    pitch = TOK_SUB

    def gather(d_ref, slot, t, j):
        return _row_copy(ys_ref, d_ref[t * TOP_K + j], rows_scr, slot * slot_tok + j * tm + t, sem.at[slot])

    def issue(d_ref, slot):
        def start(t, c):
            for j in range(TOP_K):
                gather(d_ref, slot, t, j).start(priority=j % 2)
            return c

        lax.fori_loop(0, tm, start, 0)

    def wait_slot(slot):
        pltpu.make_async_copy(ys_ref.at[pl.ds(0, slot_rows), :],
                              rows_scr.at[pl.ds(slot * slot_tok * pitch, slot_rows), :], sem.at[slot]).wait()

    def finish(slot):
        wait_slot(slot)
        d = o_ref.shape[1]
        w = V7X_LANES
        n_col = d // w
        tg = 32

        def group(gi, carry):
            g0 = pl.multiple_of(gi * tg, tg)
            grp = pl.ds(g0, tg)
            for tt in range(tg):
                for j in range(TOP_K):
                    gather(destn_ref, 1 - slot, g0 + tt, j).start(priority=j % 2)
            gates = gate_ref[grp, :]
            gb = [jnp.broadcast_to(gates[:, j:j + 1], (tg, w)) for j in range(TOP_K)]
            for s in range(TOK_SUB):
                acc_lo = acc_hi = None
                for j in range(TOP_K):
                    row0 = (slot * slot_tok + j * tm + g0) * pitch + s
                    lo, hi = _unpack_tile_rows(rows_scr[pl.ds(row0, tg, stride=pitch), :])
                    acc_lo = lo * gb[j] if acc_lo is None else acc_lo + lo * gb[j]
                    acc_hi = hi * gb[j] if acc_hi is None else acc_hi + hi * gb[j]
                for acc, c0 in ((acc_lo, s * w), (acc_hi, d // 2 + s * w)):
                    cols = slice(c0, c0 + w)
                    o_ref[grp, cols] = base_ref[grp, cols] + g2_ref[0, :, cols] * acc
            return carry

        lax.fori_loop(0, tm // tg, group, 0)
        total = o_ref[:, :w]
        for s in range(1, n_col):
            total = total + o_ref[:, s * w:(s + 1) * w]
        mu = jnp.sum(total, axis=-1, keepdims=True) / d
        sq = jnp.square(o_ref[:, :w] - mu)
        for s in range(1, n_col):
            sq = sq + jnp.square(o_ref[:, s * w:(s + 1) * w] - mu)
        rstd = lax.rsqrt(jnp.sum(sq, axis=-1, keepdims=True) / d + LN_EPS)
        for s in range(n_col):
            cols = slice(s * w, (s + 1) * w)
            o_ref[:, cols] = (o_ref[:, cols] - mu) * rstd * lng_ref[:, cols] + lnb_ref[:, cols]

    @pl.when(i == 0)
    def _():
        issue(dest_ref, 0)

    for slot in range(2):
        @pl.when(i % 2 == slot)
        def _():
            finish(slot)

            @pl.when(i == n_tiles - 1)
            def _():
                wait_slot(1 - slot)


def _combine(ys, dest_flat, gates, base, mods3, row_of_tile, ln_g, ln_b, name, tm=128):
    t, d = base.shape
    ep = gates.shape[1]
    n_tiles = t // tm
    kern = functools.partial(_combine_kernel, tm=tm, n_tiles=n_tiles)
    return pl.pallas_call(
        kern,
        grid=(n_tiles,),
        in_specs=[pl.BlockSpec((tm * TOP_K,), lambda i: (i,), memory_space=pltpu.SMEM),
                  pl.BlockSpec((tm * TOP_K,), lambda i: (jnp.minimum(i + 1, n_tiles - 1),), memory_space=pltpu.SMEM),
                  pl.BlockSpec(memory_space=pl.ANY),
                  pl.BlockSpec((tm, ep), lambda i: (i, 0)),
                  pl.BlockSpec((tm, d), lambda i: (i, 0)),
                  pl.BlockSpec((1, 1, d), lambda i: (row_of_tile(i, tm), 0, 5)),
                  _whole((1, d)), _whole((1, d))],
        out_specs=pl.BlockSpec((tm, d), lambda i: (i, 0)),
        out_shape=jax.ShapeDtypeStruct((t, d), F32),
        scratch_shapes=[pltpu.VMEM((2 * TOP_K * tm * TOK_SUB, V7X_LANES), jnp.uint32),
                        pltpu.SemaphoreType.DMA((2,))],
        compiler_params=_cparams(("arbitrary",)),
        name=name,
    )(dest_flat, dest_flat, ys, gates, base, mods3, ln_g.reshape(1, d), ln_b.reshape(1, d))


def _layer_pair(xp, xs, s_f, s_b, c, c_ctx, lp, alpha):
    bp, lp_len, d = xp.shape
    bs, ls_len, _ = xs.shape
    tp, ts = bp * lp_len, bs * ls_len
    assert lp_len % DFT_BLOCK == 0 and ls_len % DFT_BLOCK == 0 and DFT_BLOCK % (ls_len // DFT_BLOCK) == 0
    assert d == 2 * TOK_SUB * V7X_LANES
    n_exp = lp['router_w'].shape[1]
    r = MOE_ROWS

    mod_rows = 8 * ((1 + bs + 7) // 8)
    cond = jnp.zeros((mod_rows, d), F32).at[0].set(c_ctx).at[1:1 + bs].set(c)
    mods3 = _ada(cond, lp['w_ada'], lp['b_ada']).reshape(mod_rows, 1, 6 * d)
    row_ctx = lambda i, tile: 0
    row_lat = lambda i, tile: 1 + (i * tile) // ls_len

    w_in = lp['w_in'].astype(BF16)
    ep = max(n_exp, V7X_LANES)
    rw = jnp.zeros((d, ep), F32).at[:, :n_exp].set(lp['router_w'])
    rwh = rw.astype(BF16)
    rwl = (rw - rwh.astype(F32)).astype(BF16)
    rb = jnp.zeros((1, ep), F32).at[0, :n_exp].set(lp['router_b'])
    out_wts = (lp['w_out'].astype(BF16), lp['sh_w_gate'].astype(BF16), lp['sh_w_up'].astype(BF16),
               lp['sh_w_down'].astype(BF16), rwh, rwl, lp['ln1_g'], lp['ln1_b'], rb)
    cw = lp['hy_bias'].shape[1]
    hy_cols = (HY_ORDER + 1) * cw
    lg = jnp.stack([jax.nn.log_sigmoid(lp['ret_decay_f'].astype(F32)),
                    jax.nn.log_sigmoid(lp['ret_decay_b'].astype(F32))])
    dk = s_f.shape[-1]
    zero_state = jnp.zeros((bp, RET_HEADS, dk, dk), F32)

    def mixers(x2, n_seq, L, row_of_tile, s0f, s0b, cnt0, latent, tag):
        proj = _proj(x2, mods3, row_of_tile, w_in, 'proj_' + tag, tm=min(PROJ_ROWS, L if latent else x2.shape[0]))
        y_hy = _hyena(proj, n_seq, L, lp, cw, tag)
        y_ret, nf, nb_ = _retention(proj, n_seq, L, hy_cols, lg, s0f, s0b, latent, 'retention_' + tag)
        routed = _out_stage(y_hy, y_ret, x2, mods3, row_of_tile, out_wts, cnt0, alpha, n_exp, 'out_' + tag)
        return routed, nf, nb_

    (base_p, h2t_p, idx_p, gate_p, rank_p, cnt_p), nf, nb_ = mixers(
        xp.reshape(tp, d), bp, lp_len, row_ctx, zero_state, zero_state, jnp.zeros((1, ep), F32), False, 'ctx')
    (base_s, h2t_s, idx_s, gate_s, rank_s, cnt_all), _, _ = mixers(
        xs.reshape(ts, d), bs, ls_len, row_lat, s_f, s_b, cnt_p, True, 'lat')

    t = tp + ts
    counts = cnt_all[0, :n_exp].astype(jnp.int32)
    padded = (counts + r - 1) // r * r
    pend = jnp.cumsum(padded)
    pstart = pend - padded
    dest_p = (pstart[idx_p[:, :TOP_K]] + rank_p[:, :TOP_K]).reshape(tp * TOP_K)
    dest_s = (pstart[idx_s[:, :TOP_K]] + rank_s[:, :TOP_K]).reshape(ts * TOP_K)
    dest = jnp.concatenate([dest_p, dest_s])
    nb = (t * TOP_K) // r + n_exp
    blk_start = jnp.arange(nb, dtype=jnp.int32) * r
    blk_e = jnp.minimum(jnp.sum(pend[None, :] <= blk_start[:, None], axis=1), n_exp - 1).astype(jnp.int32)
    n_used = (pend[-1:] // r).astype(jnp.int32)
    zpos = jnp.concatenate([pstart + counts, padded - counts, n_used]).astype(jnp.int32)
    xsort = _dispatch(h2t_p, h2t_s, dest, zpos, nb, n_exp)
    ysort = _gmm(xsort, blk_e, n_used, lp['exp_w_gate'], lp['exp_w_up'], lp['exp_w_down'], nb)
    y_p = _combine(ysort, dest_p, gate_p, base_p, mods3, row_ctx, lp['ln2_g'], lp['ln2_b'], 'moe_combine_ctx')
    y_s = _combine(ysort, dest_s, gate_s, base_s, mods3, row_lat, lp['ln2_g'], lp['ln2_b'], 'moe_combine_lat')
    return y_p.reshape(bp, lp_len, d), y_s.reshape(bs, ls_len, d), nf, nb_


def kernel(x_prompt, x_sample, state_ret_fwd, state_ret_bwd, c, c_ctx, w_in, hy_conv_w, hy_conv_b, hy_ffn_w1, hy_ffn_b1, hy_ffn_w2, hy_ffn_b2, hy_ffn_w3, hy_sin_freq, hy_bias, ret_decay_f, ret_decay_b, w_out, w_ada, b_ada, ln1_g, ln1_b, ln2_g, ln2_b, router_w, router_b, exp_w_gate, exp_w_up, exp_w_down, sh_w_gate, sh_w_up, sh_w_down):
    depth = w_in.shape[0]
    alpha = (2.0 * depth) ** 0.25
    params = dict(w_in=w_in, hy_conv_w=hy_conv_w, hy_conv_b=hy_conv_b, hy_ffn_w1=hy_ffn_w1, hy_ffn_b1=hy_ffn_b1,
                  hy_ffn_w2=hy_ffn_w2, hy_ffn_b2=hy_ffn_b2, hy_ffn_w3=hy_ffn_w3, hy_sin_freq=hy_sin_freq,
                  hy_bias=hy_bias, ret_decay_f=ret_decay_f, ret_decay_b=ret_decay_b, w_out=w_out, w_ada=w_ada,
                  b_ada=b_ada, ln1_g=ln1_g, ln1_b=ln1_b, ln2_g=ln2_g, ln2_b=ln2_b, router_w=router_w,
                  router_b=router_b, exp_w_gate=exp_w_gate, exp_w_up=exp_w_up, exp_w_down=exp_w_down,
                  sh_w_gate=sh_w_gate, sh_w_up=sh_w_up, sh_w_down=sh_w_down)
    y_p, y_s = x_prompt, x_sample
    new_f, new_b = [], []
    for l in range(depth):
        lp = {k: v[l] for k, v in params.items()}
        y_p, y_s, s_f, s_b = _layer_pair(y_p, y_s, state_ret_fwd[:, l], state_ret_bwd[:, l], c, c_ctx, lp, alpha)
        new_f.append(s_f.astype(x_prompt.dtype))
        new_b.append(s_b.astype(x_prompt.dtype))
    return (y_p, y_s, jnp.stack(new_f, axis=1), jnp.stack(new_b, axis=1))
```

```python
import functools
import math

import numpy as np
import jax
import jax.numpy as jnp
from jax import lax
from jax.experimental import pallas as pl
from jax.experimental.pallas import tpu as pltpu

F32 = jnp.float32
BF16 = jnp.bfloat16

RET_HEADS = 8
TOP_K = 8
GRID_W = 64
HY_ORDER = 2
HY_BANDS = 16
HY_TARGET = 1e-2
HY_FAST = 0.3
HY_SLOW = 1.5
ROUTED_SCALE = 2.5
ROPE_BASE = 10000.0
LN_EPS = 1e-5
GN_EPS = 1e-6

V7X_LANES = 128
V7X_VMEM_LIMIT_BYTES = 56 * 1024 * 1024

DFT_BLOCK = 256
RET_CHUNK = 256
PROJ_ROWS = 1024
MOE_ROWS = 512
TOK_SUB = 8


def _cparams(sem, vmem=V7X_VMEM_LIMIT_BYTES):
    return pltpu.CompilerParams(dimension_semantics=sem, vmem_limit_bytes=vmem)


def _dot(a, b):
    return jnp.dot(a, b, preferred_element_type=F32)


def _split_bf16(a):
    hi = a.astype(BF16)
    lo = (a - hi.astype(F32)).astype(BF16)
    return hi, lo


def _dot3(a, b):
    ah, al = _split_bf16(a)
    bh, bl = _split_bf16(b)
    return _dot(ah, bh) + _dot(ah, bl) + _dot(al, bh)


def _silu(x):
    return x * jax.nn.sigmoid(x)


def _whole(shape):
    n = len(shape)
    return pl.BlockSpec(shape, lambda *_: (0,) * n)


def _ada_kernel(c_ref, w_ref, b_ref, o_ref):
    o_ref[...] = _dot3(_silu(c_ref[...]), w_ref[...]) + b_ref[...]


def _ada(cond, w_ada, b_ada, tn=1024):
    rows, d = cond.shape
    n = w_ada.shape[1]
    return pl.pallas_call(
        _ada_kernel,
        grid=(n // tn,),
        in_specs=[pl.BlockSpec((rows, d), lambda j: (0, 0)),
                  pl.BlockSpec((d, tn), lambda j: (0, j)),
                  pl.BlockSpec((1, tn), lambda j: (0, j))],
        out_specs=pl.BlockSpec((rows, tn), lambda j: (0, j)),
        out_shape=jax.ShapeDtypeStruct((rows, n), F32),
        compiler_params=_cparams(("parallel",)),
        name='ada_mods',
    )(cond, w_ada, b_ada.reshape(1, n))


def _proj_kernel(x_ref, sc_ref, sh_ref, w_ref, o_ref, h_scr):
    @pl.when(pl.program_id(1) == 0)
    def _():
        h_scr[...] = (x_ref[...] * (1.0 + sc_ref[0]) + sh_ref[0]).astype(BF16)

    o_ref[...] = _dot(h_scr[...], w_ref[...]).astype(o_ref.dtype)


def _proj(x, mods3, row_of_tile, w_in_bf16, name, tm, tn=1024):
    t, d = x.shape
    n = w_in_bf16.shape[1]
    row_of_tile = functools.partial(row_of_tile, tile=tm)
    return pl.pallas_call(
        _proj_kernel,
        name=name,
        grid=(t // tm, n // tn),
        in_specs=[pl.BlockSpec((tm, d), lambda i, j: (i, 0)),
                  pl.BlockSpec((1, 1, d), lambda i, j: (row_of_tile(i), 0, 1)),
                  pl.BlockSpec((1, 1, d), lambda i, j: (row_of_tile(i), 0, 0)),
                  pl.BlockSpec((d, tn), lambda i, j: (0, j))],
        out_specs=pl.BlockSpec((tm, tn), lambda i, j: (i, j)),
        out_shape=jax.ShapeDtypeStruct((t, n), BF16),
        scratch_shapes=[pltpu.VMEM((tm, d), BF16)],
        compiler_params=_cparams(("parallel", "arbitrary")),
    )(x, mods3, mods3, w_in_bf16)


@functools.lru_cache(maxsize=None)
def _hy_mats(L):
    nn2 = L // DFT_BLOCK
    n_full = 2 * L
    n2_full = 2 * nn2
    g_rows = DFT_BLOCK // nn2
    k2 = np.arange(nn2)
    n2 = np.arange(nn2)
    s1 = np.zeros((nn2, DFT_BLOCK, DFT_BLOCK), np.complex128)
    for g in range(nn2):
        for s in range(g_rows):
            n1 = g * g_rows + s
            blk = (np.exp(-2j * np.pi * np.outer(k2 + 0.5, n2) / n2_full)
                   * np.exp(-2j * np.pi * (k2 + 0.5) * n1 / n_full)[:, None])
            s1[g][np.ix_(k2 * g_rows + s, n2 * g_rows + s)] = blk
    m1f = np.concatenate([s1.real, s1.imag], axis=1)
    a = np.conj(s1).transpose(0, 2, 1)
    m1i = (2.0 / n_full) * np.concatenate([a.real, -a.imag], axis=2)
    k1 = np.arange(DFT_BLOCK)
    f = np.exp(-2j * np.pi * np.outer(k1, k1) / DFT_BLOCK)
    f2f = np.block([[f.real, -f.imag], [f.imag, f.real]])
    f2i = np.block([[f.real, f.imag], [-f.imag, f.real]])
    return tuple(np.asarray(m, np.float32) for m in (m1f, m1i, f2f, f2i))


def _hy_unroll(nn2):
    return 4 if nn2 % 4 == 0 else 1


def _hy_fwd_stage1(u_scr, m1f_ref, v_scr, nn2):
    g_rows = DFT_BLOCK // nn2

    def body(g, carry):
        off = pl.multiple_of(g * g_rows, g_rows)
        parts = [u_scr[pl.ds(n2 * DFT_BLOCK + off, g_rows), :] for n2 in range(nn2)]
        inp = jnp.concatenate(parts, axis=0).astype(BF16)
        out = _dot(m1f_ref[g], inp)
        for ri in range(2):
            for k2 in range(nn2):
                r0 = (ri * nn2 + k2) * g_rows
                v_scr[k2, pl.ds(ri * DFT_BLOCK + off, g_rows), :] = out[r0:r0 + g_rows].astype(BF16)
        return carry

    lax.fori_loop(0, nn2, body, 0, unroll=2 if nn2 % 2 == 0 else 1)


def _hy_inv_stage1(w_scr, m1i_ref, y_scr, nn2):
    g_rows = DFT_BLOCK // nn2

    def body(g, carry):
        off = pl.multiple_of(g * g_rows, g_rows)
        parts = [w_scr[k2, pl.ds(ri * DFT_BLOCK + off, g_rows), :]
                 for ri in range(2) for k2 in range(nn2)]
        inp = jnp.concatenate(parts, axis=0)
        out = _dot(m1i_ref[g], inp)
        for n2 in range(nn2):
            y_scr[pl.ds(n2 * DFT_BLOCK + off, g_rows), :] = out[n2 * g_rows:(n2 + 1) * g_rows]
        return carry

    lax.fori_loop(0, nn2, body, 0, unroll=2 if nn2 % 2 == 0 else 1)


def _short_conv(u, w, b):
    L = u.shape[0]
    row = lax.broadcasted_iota(jnp.int32, u.shape, 0)
    prev = jnp.where(row == 0, 0.0, pltpu.roll(u, 1, 0))
    nxt = jnp.where(row == L - 1, 0.0, pltpu.roll(u, L - 1, 0))
    return prev * w[0:1] + u * w[1:2] + nxt * w[2:3] + b


def _hy_filter_kernel(z_ref, w1_ref, b1_ref, w2_ref, b2_ref, w3_ref, fr_ref, dl_ref, o_ref, *, n_half_tiles, c):
    i = pl.program_id(0)
    z = z_ref[...]
    fr = fr_ref[...]
    h = jnp.sin(fr * (_dot3(z, w1_ref[...]) + b1_ref[...]))
    h = jnp.sin(fr * (_dot3(h, w2_ref[...]) + b2_ref[...]))
    h = _dot3(h, w3_ref[...])
    tn = z[:, 0:1]
    decay = jnp.exp(-tn * jnp.abs(dl_ref[...]))
    valid = z[:, 2 * HY_BANDS + 1:2 * HY_BANDS + 2]
    second = i >= n_half_tiles
    for o in range(HY_ORDER):
        fwd = h[:, (2 * o) * c:(2 * o + 1) * c]
        bwd = h[:, (2 * o + 1) * c:(2 * o + 2) * c]
        val = jnp.where(second, -bwd, fwd) * decay * valid
        o_ref[:, o * c:(o + 1) * c] = val


def _hy_filters(L, w1, b1, w2, b2, w3, freq, name, rt=256):
    c = w3.shape[1] // (2 * HY_ORDER)
    fh = w1.shape[1]
    r = np.arange(2 * L)
    t = np.where(r < L, r, 2 * L - r).astype(np.float64)
    f = np.linspace(1e-4, HY_BANDS - 1, HY_BANDS)
    ang = (2.0 * math.pi / L) * t[:, None] * f[None, :]
    z = np.zeros((2 * L, V7X_LANES), np.float32)
    z[:, 0] = t / (L - 1)
    z[:, 1:1 + HY_BANDS] = np.cos(ang)
    z[:, 1 + HY_BANDS:1 + 2 * HY_BANDS] = -np.sin(ang)
    z[:, 1 + 2 * HY_BANDS] = (r != L)
    w1p = jnp.zeros((V7X_LANES, fh), F32).at[:w1.shape[0]].set(w1)
    deltas = np.linspace(math.log(HY_TARGET) / HY_SLOW, math.log(HY_TARGET) / HY_FAST, c).astype(np.float32)
    kern = functools.partial(_hy_filter_kernel, n_half_tiles=L // rt, c=c)
    return pl.pallas_call(
        kern,
        grid=(2 * L // rt,),
        in_specs=[pl.BlockSpec((rt, V7X_LANES), lambda i: (i, 0)),
                  _whole((V7X_LANES, fh)), _whole((1, fh)), _whole((fh, fh)), _whole((1, fh)),
                  _whole(w3.shape), _whole((1, fh)), _whole((1, c))],
        out_specs=pl.BlockSpec((rt, HY_ORDER * c), lambda i: (i, 0)),
        out_shape=jax.ShapeDtypeStruct((2 * L, HY_ORDER * c), F32),
        compiler_params=_cparams(("parallel",)),
        name=name,
    )(jnp.asarray(z), w1p, b1.reshape(1, fh), w2, b2.reshape(1, fh), w3, freq.reshape(1, fh),
      jnp.asarray(deltas).reshape(1, c))


def _hy_spec_kernel(f_ref, m1f_ref, f2f_ref, o_ref, u_scr, v_scr, acc_scr, *, nn2):
    half = pl.program_id(1)
    u_scr[...] = f_ref[...]
    _hy_fwd_stage1(u_scr, m1f_ref, v_scr, nn2)

    def body(k2, carry):
        z = _dot(f2f_ref[...], v_scr[k2])

        @pl.when(half == 0)
        def _():
            acc_scr[k2] = z

        @pl.when(half == 1)
        def _():
            sign = 1.0 - 2.0 * (k2 % 2)
            a = acc_scr[k2]
            o_ref[0, k2] = (a[:DFT_BLOCK] + sign * z[DFT_BLOCK:]).astype(o_ref.dtype)
            o_ref[1, k2] = (a[DFT_BLOCK:] - sign * z[:DFT_BLOCK]).astype(o_ref.dtype)

        return carry

    lax.fori_loop(0, nn2, body, 0, unroll=2 if nn2 % 2 == 0 else 1)


def _hy_spectrum(filt, L, name, ct=256):
    nn2 = L // DFT_BLOCK
    cols = filt.shape[1]
    m1f, _, f2f, _ = _hy_mats(L)
    kern = functools.partial(_hy_spec_kernel, nn2=nn2)
    return pl.pallas_call(
        kern,
        grid=(cols // ct, 2),
        in_specs=[pl.BlockSpec((L, ct), lambda j, h: (h, j)),
                  pl.BlockSpec(memory_space=pltpu.VMEM),
                  pl.BlockSpec(memory_space=pltpu.VMEM)],
        out_specs=pl.BlockSpec((2, nn2, DFT_BLOCK, ct), lambda j, h: (0, 0, 0, j)),
        out_shape=jax.ShapeDtypeStruct((2, nn2, DFT_BLOCK, cols), BF16),
        scratch_shapes=[pltpu.VMEM((L, ct), F32),
                        pltpu.VMEM((nn2, 2 * DFT_BLOCK, ct), BF16),
                        pltpu.VMEM((nn2, 2 * DFT_BLOCK, ct), F32)],
        compiler_params=_cparams(("parallel", "arbitrary")),
        name=name,
    )(filt, jnp.asarray(m1f, BF16), jnp.asarray(f2f, BF16))


def _hy_conv_kernel(u_ref, g_ref, cwu_ref, cbu_ref, cwg_ref, cbg_ref, bias_ref, ks_ref,
                    m1f_ref, m1i_ref, f2f_ref, f2i_ref, o_ref, u_scr, v_scr, w_scr, y_scr, *, nn2, conv_u):
    u = u_ref[...].astype(F32)
    if conv_u:
        u = _short_conv(u, cwu_ref[...], cbu_ref[...])
    u_scr[...] = u
    _hy_fwd_stage1(u_scr, m1f_ref, v_scr, nn2)

    def body(k2, carry):
        z = _dot(f2f_ref[...], v_scr[k2])
        zr, zi = z[:DFT_BLOCK], z[DFT_BLOCK:]
        kr = ks_ref[0, k2].astype(F32)
        ki = ks_ref[1, k2].astype(F32)
        y = jnp.concatenate([zr * kr - zi * ki, zr * ki + zi * kr], axis=0).astype(BF16)
        w_scr[k2] = _dot(f2i_ref[...], y).astype(BF16)
        return carry

    lax.fori_loop(0, nn2, body, 0, unroll=_hy_unroll(nn2))
    _hy_inv_stage1(w_scr, m1i_ref, y_scr, nn2)
    gate = _short_conv(g_ref[...].astype(F32), cwg_ref[...], cbg_ref[...])
    o_ref[...] = (gate * (y_scr[...] + bias_ref[...] * u_scr[...])).astype(o_ref.dtype)


def _hy_conv(u_arr, u_col0, conv_u, proj, g_col0, n_seq, L, conv_w, conv_b, bias_o, spec, spec_col0, c, name, ct=256):
    nn2 = L // DFT_BLOCK
    m1f, m1i, f2f, f2i = (jnp.asarray(m, BF16) for m in _hy_mats(L))
    nct = c // ct
    ub, gb, sb = u_col0 // ct, g_col0 // ct, spec_col0 // ct
    kern = functools.partial(_hy_conv_kernel, nn2=nn2, conv_u=conv_u)
    vm = pl.BlockSpec(memory_space=pltpu.VMEM)
    return pl.pallas_call(
        kern,
        grid=(nct, n_seq),
        in_specs=[pl.BlockSpec((L, ct), lambda j, b: (b, ub + j)),
                  pl.BlockSpec((L, ct), lambda j, b: (b, gb + j)),
                  pl.BlockSpec((3, ct), lambda j, b: (0, ub + j)),
                  pl.BlockSpec((1, ct), lambda j, b: (0, ub + j)),
                  pl.BlockSpec((3, ct), lambda j, b: (0, gb + j)),
                  pl.BlockSpec((1, ct), lambda j, b: (0, gb + j)),
                  pl.BlockSpec((1, ct), lambda j, b: (0, j)),
                  pl.BlockSpec((2, nn2, DFT_BLOCK, ct), lambda j, b: (0, 0, 0, sb + j)),
                  vm, vm, vm, vm],
        out_specs=pl.BlockSpec((L, ct), lambda j, b: (b, j)),
        out_shape=jax.ShapeDtypeStruct((n_seq * L, c), BF16),
        scratch_shapes=[pltpu.VMEM((L, ct), F32),
                        pltpu.VMEM((nn2, 2 * DFT_BLOCK, ct), BF16),
                        pltpu.VMEM((nn2, 2 * DFT_BLOCK, ct), BF16),
                        pltpu.VMEM((L, ct), F32)],
        compiler_params=_cparams(("parallel", "parallel")),
        name=name,
    )(u_arr, proj, conv_w, conv_b, conv_w, conv_b, bias_o, spec, m1f, m1i, f2f, f2i)


def _hyena(proj, n_seq, L, lp, c, tag):
    filt = _hy_filters(L, lp['hy_ffn_w1'], lp['hy_ffn_b1'], lp['hy_ffn_w2'], lp['hy_ffn_b2'],
                       lp['hy_ffn_w3'], lp['hy_sin_freq'], 'hy_filter_' + tag)
    spec = _hy_spectrum(filt, L, 'hy_spectrum_' + tag)
    cw = lp['hy_conv_w']
    cb = lp['hy_conv_b'].reshape(1, -1)
    bias = lp['hy_bias']
    z = _hy_conv(proj, 0, True, proj, c, n_seq, L, cw, cb, bias[0:1], spec, 0, c, 'hy_conv1_' + tag)
    z = _hy_conv(z, 0, False, proj, 2 * c, n_seq, L, cw, cb, bias[1:2], spec, c, c, 'hy_conv2_' + tag)
    return z


def _rope_tables(L, dk):
    rows = L // GRID_W
    row = np.repeat(np.arange(rows), GRID_W).astype(np.float64)
    col = np.tile(np.arange(GRID_W), rows).astype(np.float64)
    quarter = dk // 4
    inv = ROPE_BASE ** (-np.arange(quarter, dtype=np.float64) / quarter)
    ar = row[:, None] * inv[None, :]
    ac = col[:, None] * inv[None, :]
    cos = np.concatenate([np.cos(ar), np.cos(ar), np.cos(ac), np.cos(ac)], axis=1)
    sin = np.concatenate([-np.sin(ar), np.sin(ar), -np.sin(ac), np.sin(ac)], axis=1)
    return jnp.asarray(cos, F32), jnp.asarray(sin, F32)


def _rope(x, cos, sin_signed):
    dk = x.shape[1]
    q = dk // 4
    lane = lax.broadcasted_iota(jnp.int32, x.shape, 1)
    first = (lane % (2 * q)) < q
    swapped = jnp.where(first, pltpu.roll(x, dk - q, 1), pltpu.roll(x, q, 1))
    return x * cos + swapped * sin_signed


def _ret_kernel(lg_ref, q_ref, k_ref, v_ref, g_ref, cos_ref, sin_ref, s0f_ref, s0b_ref,
                y_ref, sf_ref, sb_ref, qd_scr, u_scr, s_scr, o_scr, *, latent, chunk):
    h = pl.program_id(1)
    L, dk = q_ref.shape
    n = L // chunk
    lgf = lg_ref[0, h]
    lgb = lg_ref[1, h]
    pos = lax.broadcasted_iota(jnp.int32, (chunk, dk), 0).astype(F32)
    qd_f = jnp.exp(lgf * (pos + 1.0))
    qd_b = jnp.exp(lgb * (chunk - pos))
    kd_f = jnp.exp(lgf * (chunk - 1.0 - pos))
    kd_b = jnp.exp(lgb * pos)
    pos_r = lax.broadcasted_iota(jnp.int32, (chunk, chunk), 0).astype(F32)
    pos_c = lax.broadcasted_iota(jnp.int32, (chunk, chunk), 1).astype(F32)
    diff = pos_r - pos_c
    mask = (jnp.where(diff >= 0, jnp.exp(lgf * jnp.maximum(diff, 0.0)), 0.0)
            + jnp.where(diff <= 0, jnp.exp(lgb * jnp.maximum(-diff, 0.0)), 0.0))
    unroll = 2 if n % 2 == 0 else 1

    def intra(i, carry):
        r0 = pl.multiple_of(i * chunk, chunk)
        rows = pl.ds(r0, chunk)
        q = q_ref[rows, :].astype(F32)
        k = k_ref[rows, :].astype(F32) * (dk ** -0.5)
        if latent:
            q = _rope(q, cos_ref[rows, :], sin_ref[rows, :])
            k = _rope(k, cos_ref[rows, :], sin_ref[rows, :])
        qd_scr[rows, :dk] = (q * qd_f).astype(BF16)
        qd_scr[rows, dk:] = (q * qd_b).astype(BF16)
        vi = v_ref[rows, :]
        att = lax.dot_general(q.astype(BF16), k.astype(BF16), (((1,), (1,)), ((), ())),
                              preferred_element_type=F32) * mask
        o_scr[rows, :] = _dot(att.astype(BF16), vi)
        kd = jnp.concatenate([k * kd_f, k * kd_b], axis=1)
        u_scr[i] = _dot(kd.T.astype(BF16), vi)
        return carry

    lax.fori_loop(0, n, intra, 0, unroll=unroll)

    c_f = jnp.exp(lgf * chunk)
    c_b = jnp.exp(lgb * chunk)

    def fwd_state(i, s):
        s_scr[i, :dk, :] = s.astype(BF16)
        return c_f * s + u_scr[i, :dk, :]

    def bwd_state(j, s):
        i = n - 1 - j
        s_scr[i, dk:, :] = s.astype(BF16)
        return c_b * s + u_scr[i, dk:, :]

    sf_ref[0, 0] = lax.fori_loop(0, n, fwd_state, s0f_ref[0, 0])
    sb_ref[0, 0] = lax.fori_loop(0, n, bwd_state, s0b_ref[0, 0])

    def inter(i, carry):
        rows = pl.ds(pl.multiple_of(i * chunk, chunk), chunk)
        o = o_scr[rows, :] + _dot(qd_scr[rows, :], s_scr[i])
        mu = jnp.mean(o, axis=-1, keepdims=True)
        var = jnp.mean(jnp.square(o - mu), axis=-1, keepdims=True)
        o = (o - mu) * lax.rsqrt(var + GN_EPS)
        y_ref[rows, :] = (o * _silu(g_ref[rows, :].astype(F32))).astype(y_ref.dtype)
        return carry

    lax.fori_loop(0, n, inter, 0, unroll=unroll)


def _retention(proj, n_seq, L, col0, lg, s0_f, s0_b, latent, name):
    dk = s0_f.shape[-1]
    hb = col0 // dk
    chunk = min(RET_CHUNK, L)
    cos, sin = _rope_tables(L, dk) if latent else (jnp.zeros((8, dk), F32), jnp.zeros((8, dk), F32))
    tbl = pl.BlockSpec(cos.shape, lambda b, h: (0, 0))
    seq = lambda part: pl.BlockSpec((L, dk), lambda b, h: (b, hb + part * RET_HEADS + h))
    st = pl.BlockSpec((1, 1, dk, dk), lambda b, h: (b, h, 0, 0))
    kern = functools.partial(_ret_kernel, latent=latent, chunk=chunk)
    return pl.pallas_call(
        kern,
        grid=(n_seq, RET_HEADS),
        in_specs=[pl.BlockSpec(memory_space=pltpu.SMEM),
                  seq(0), seq(1), seq(2), seq(3), tbl, tbl, st, st],
        out_specs=[pl.BlockSpec((L, dk), lambda b, h: (b, h)), st, st],
        out_shape=[jax.ShapeDtypeStruct((n_seq * L, RET_HEADS * dk), BF16),
                   jax.ShapeDtypeStruct((n_seq, RET_HEADS, dk, dk), F32),
                   jax.ShapeDtypeStruct((n_seq, RET_HEADS, dk, dk), F32)],
        scratch_shapes=[pltpu.VMEM((L, 2 * dk), BF16),
                        pltpu.VMEM((L // chunk, 2 * dk, dk), F32), pltpu.VMEM((L // chunk, 2 * dk, dk), BF16),
                        pltpu.VMEM((L, dk), F32)],
        compiler_params=_cparams(("parallel", "parallel")),
        name=name,
    )(lg, proj, proj, proj, proj, cos, sin, s0_f, s0_b)


def _layernorm(r, g, b):
    mu = jnp.mean(r, axis=-1, keepdims=True)
    var = jnp.mean(jnp.square(r - mu), axis=-1, keepdims=True)
    return (r - mu) * lax.rsqrt(var + LN_EPS) * g + b


def _route(logits, router_b, tri, before0, n_exp):
    tm, ep = logits.shape
    lane = lax.broadcasted_iota(jnp.int32, (tm, ep), 1).astype(F32)
    scores = jax.nn.sigmoid(logits)
    work = jnp.where(lane < n_exp, scores + router_b, -jnp.inf)
    sel = jnp.zeros((tm, ep), jnp.bool_)
    idx_cols = []
    s_cols = []
    for _ in range(TOP_K):
        m = jnp.max(work, axis=-1, keepdims=True)
        first = jnp.min(jnp.where(work == m, lane, float(ep)), axis=-1, keepdims=True)
        hit = lane == first
        idx_cols.append(first)
        s_cols.append(jnp.sum(jnp.where(hit, scores, 0.0), axis=-1, keepdims=True))
        sel = jnp.logical_or(sel, hit)
        work = jnp.where(hit, -jnp.inf, work)
    denom = functools.reduce(lambda a, b: a + b, s_cols)
    before = _dot(tri, sel.astype(BF16)) + before0
    idx_out = jnp.zeros((tm, ep), F32)
    gate_out = jnp.zeros((tm, ep), F32)
    rank_out = jnp.zeros((tm, ep), jnp.int32)
    for j in range(TOP_K):
        hit = lane == idx_cols[j]
        rank_j = jnp.sum(jnp.where(hit, before, 0.0), axis=-1, keepdims=True).astype(jnp.int32)
        idx_out = jnp.where(lane == j, idx_cols[j], idx_out)
        gate_out = jnp.where(lane == j, ROUTED_SCALE * s_cols[j] / denom, gate_out)
        rank_out = jnp.where(lane == j, rank_j, rank_out)
    counts = before0 + jnp.sum(sel.astype(F32), axis=0, keepdims=True)
    return idx_out.astype(jnp.int32), gate_out, rank_out, counts


def _out_kernel(yh_ref, yr_ref, x_ref, g1_ref, sc2_ref, sh2_ref, g2_ref, lng_ref, lnb_ref,
                wo_ref, wg_ref, wu_ref, wd_ref, rwh_ref, rwl_ref, rb_ref, tri_ref, cnt0_ref,
                base_ref, h2_ref, idx_ref, gate_ref, rank_ref, cnt_ref, *, alpha, n_exp):
    @pl.when(pl.program_id(0) == 0)
    def _():
        cnt_ref[...] = cnt0_ref[...]

    c = yh_ref.shape[1]
    mix = _dot(yh_ref[...], wo_ref[:c]) + _dot(yr_ref[...], wo_ref[c:])
    x1 = _layernorm(alpha * x_ref[...] + g1_ref[0] * mix, lng_ref[...], lnb_ref[...])
    h2 = x1 * (1.0 + sc2_ref[0]) + sh2_ref[0]
    _to_token_tiles(h2_ref, h2)
    hh, hl = _split_bf16(h2)
    logits = _dot(hh, rwh_ref[...]) + _dot(hh, rwl_ref[...]) + _dot(hl, rwh_ref[...])
    a = (_silu(_dot(hh, wg_ref[...])) * _dot(hh, wu_ref[...])).astype(BF16)
    base_ref[...] = alpha * x1 + g2_ref[0] * _dot(a, wd_ref[...])
    idx_ref[...], gate_ref[...], rank_ref[...], cnt_ref[...] = _route(
        logits, rb_ref[...], tri_ref[...], cnt_ref[...], n_exp)


def _out_stage(y_hy, y_ret, x, mods3, row_of_tile, wts, cnt0, alpha, n_exp, name, tm=256):
    t, d = x.shape
    c = y_hy.shape[1]
    wo, sg, su, sd, rwh, rwl, ln_g, ln_b, rb = wts
    ep = rwh.shape[1]
    tri = jnp.asarray(np.tril(np.ones((tm, tm), np.float32), -1), BF16)
    row_of_tile = functools.partial(row_of_tile, tile=tm)
    mod = lambda k: pl.BlockSpec((1, 1, d), lambda i: (row_of_tile(i), 0, k))
    vm = pl.BlockSpec(memory_space=pltpu.VMEM)
    tile = lambda w: pl.BlockSpec((tm, w), lambda i: (i, 0))
    kern = functools.partial(_out_kernel, alpha=alpha, n_exp=n_exp)
    return pl.pallas_call(
        kern,
        grid=(t // tm,),
        in_specs=[tile(c), tile(c), tile(d), mod(2), mod(4), mod(3), mod(5),
                  _whole((1, d)), _whole((1, d)), vm, vm, vm, vm, vm, vm, vm, vm, vm],
        out_specs=[tile(d), pl.BlockSpec((tm * TOK_SUB, V7X_LANES), lambda i: (i, 0)),
                   tile(ep), tile(ep), tile(ep), _whole((1, ep))],
        out_shape=[jax.ShapeDtypeStruct((t, d), F32), jax.ShapeDtypeStruct((t * TOK_SUB, V7X_LANES), jnp.uint32),
                   jax.ShapeDtypeStruct((t, ep), jnp.int32), jax.ShapeDtypeStruct((t, ep), F32),
                   jax.ShapeDtypeStruct((t, ep), jnp.int32), jax.ShapeDtypeStruct((1, ep), F32)],
        compiler_params=_cparams(("arbitrary",)),
        name=name,
    )(y_hy, y_ret, x, mods3, mods3, mods3, mods3, ln_g.reshape(1, d), ln_b.reshape(1, d),
      wo, sg, su, sd, rwh, rwl, rb, tri, cnt0)


_HI_MASK = np.uint32(0xFFFF0000)


def _bf16_bits(v):
    return lax.bitcast_convert_type(v.astype(BF16).astype(F32), jnp.uint32)


def _to_token_tiles(ref, val):
    rows, d = val.shape
    half = d // 2
    for s in range(TOK_SUB):
        lo = _bf16_bits(val[:, s * V7X_LANES:(s + 1) * V7X_LANES]) >> 16
        hi = _bf16_bits(val[:, half + s * V7X_LANES:half + (s + 1) * V7X_LANES]) & _HI_MASK
        ref[pl.ds(s, rows, stride=TOK_SUB), :] = lo | hi


def _unpack_tile_rows(words):
    return (lax.bitcast_convert_type(words << 16, F32), lax.bitcast_convert_type(words & _HI_MASK, F32))


def _from_token_tiles(ref, row0, rows):
    parts = [_unpack_tile_rows(ref[pl.ds(row0 * TOK_SUB + s, rows, stride=TOK_SUB), :]) for s in range(TOK_SUB)]
    return jnp.concatenate([p[0] for p in parts] + [p[1] for p in parts], axis=1)


def _row_copy(src_ref, src_row, dst_ref, dst_row, sem):
    src = src_ref.at[pl.ds(pl.multiple_of(src_row * TOK_SUB, TOK_SUB), TOK_SUB), :]
    dst = dst_ref.at[pl.ds(pl.multiple_of(dst_row * TOK_SUB, TOK_SUB), TOK_SUB), :]
    return pltpu.make_async_copy(src, dst, sem)


def _dispatch_kernel(dest_ref, zpos_ref, hc_ref, hl_ref, xs_ref, zero_scr, sem, *, n_exp, tm, ctx_tiles, nb):
    i = pl.program_id(0)

    @pl.when(i == 0)
    def _():
        zero_scr[...] = jnp.zeros_like(zero_scr)
        max_piece = zero_scr.shape[0] // TOK_SUB
        pieces = [max_piece >> k for k in range(max_piece.bit_length())]

        def zero_copy(slot, n_tok):
            dst = xs_ref.at[pl.ds(pl.multiple_of(slot * TOK_SUB, TOK_SUB), n_tok * TOK_SUB), :]
            return pltpu.make_async_copy(zero_scr.at[pl.ds(0, n_tok * TOK_SUB), :], dst, sem)

        def each_pad_piece(e, fn):
            pad = zpos_ref[n_exp + e]
            for b in pieces:
                @pl.when((pad & b) != 0)
                def _():
                    fn(zero_copy(zpos_ref[e] + (pad & (-2 * b)), b))

        def each_tail_piece(b, fn):
            for half in range(MOE_ROWS // max_piece):
                fn(zero_copy(b * MOE_ROWS + half * max_piece, max_piece))

        def run(fn):
            lax.fori_loop(0, n_exp, lambda e, c: (each_pad_piece(e, fn), c)[1], 0)
            lax.fori_loop(zpos_ref[2 * n_exp], nb, lambda b, c: (each_tail_piece(b, fn), c)[1], 0)

        run(lambda cp: cp.start())
        run(lambda cp: cp.wait())

    def scatter(h_ref):
        def start(t, c):
            for j in range(TOP_K):
                _row_copy(h_ref, t, xs_ref, dest_ref[t * TOP_K + j], sem).start(priority=j % 2)
            return c

        lax.fori_loop(0, tm, start, 0)
        for _ in range(TOP_K):
            pltpu.make_async_copy(h_ref, xs_ref.at[pl.ds(0, tm * TOK_SUB), :], sem).wait()

    @pl.when(i < ctx_tiles)
    def _():
        scatter(hc_ref)

    @pl.when(i >= ctx_tiles)
    def _():
        scatter(hl_ref)


def _dispatch(h2t_ctx, h2t_lat, dest_flat, zpos, nb, n_exp, tm=512):
    w = h2t_ctx.shape[1]
    ctx_tiles = h2t_ctx.shape[0] // TOK_SUB // tm
    lat_tiles = h2t_lat.shape[0] // TOK_SUB // tm
    kern = functools.partial(_dispatch_kernel, n_exp=n_exp, tm=tm, ctx_tiles=ctx_tiles, nb=nb)
    return pl.pallas_call(
        kern,
        grid=(ctx_tiles + lat_tiles,),
        in_specs=[pl.BlockSpec((tm * TOP_K,), lambda i: (i,), memory_space=pltpu.SMEM),
                  pl.BlockSpec(memory_space=pltpu.SMEM),
                  pl.BlockSpec((tm * TOK_SUB, w), lambda i: (jnp.minimum(i, ctx_tiles - 1), 0)),
                  pl.BlockSpec((tm * TOK_SUB, w), lambda i: (jnp.maximum(i - ctx_tiles, 0), 0))],
        out_specs=pl.BlockSpec(memory_space=pl.ANY),
        out_shape=jax.ShapeDtypeStruct((nb * MOE_ROWS * TOK_SUB, w), jnp.uint32),
        scratch_shapes=[pltpu.VMEM((MOE_ROWS // 2 * TOK_SUB, w), jnp.uint32), pltpu.SemaphoreType.DMA(())],
        compiler_params=_cparams(("arbitrary",)),
        name='moe_dispatch',
    )(dest_flat, zpos, h2t_ctx, h2t_lat)


def _gmm_kernel(be_ref, nu_ref, x_ref, wg_ref, wu_ref, wd_ref, o_ref, wg_scr, wu_scr, wd_scr, y_scr):
    b = pl.program_id(0)
    rows = x_ref.shape[0] // TOK_SUB
    n_used = nu_ref[0]

    @pl.when(b == 0)
    def _():
        y_scr[...] = jnp.zeros_like(y_scr)

    @pl.when(b < n_used)
    def _():
        @pl.when(jnp.logical_or(b == 0, be_ref[b] != be_ref[jnp.maximum(b - 1, 0)]))
        def _():
            wg_scr[...] = wg_ref[0].astype(BF16)
            wu_scr[...] = wu_ref[0].astype(BF16)
            wd_scr[...] = wd_ref[0].astype(BF16)

        _to_token_tiles(o_ref, y_scr[...])
        x = _from_token_tiles(x_ref, 0, rows).astype(BF16)
        a = (_silu(_dot(x, wg_scr[...])) * _dot(x, wu_scr[...])).astype(BF16)
        y_scr[...] = _dot(a, wd_scr[...])

    @pl.when(b == n_used)
    def _():
        _to_token_tiles(o_ref, y_scr[...])

    @pl.when(b > n_used)
    def _():
        o_ref[...] = jnp.zeros_like(o_ref)


def _gmm(xs, blk_e, n_used, wg, wu, wd, nb):
    w = xs.shape[1]
    d, f = wg.shape[1], wg.shape[2]
    r8 = MOE_ROWS * TOK_SUB
    last = lambda b, nu: jnp.minimum(b, nu[0] - 1)
    grid_spec = pltpu.PrefetchScalarGridSpec(
        num_scalar_prefetch=2,
        grid=(nb + 1,),
        in_specs=[pl.BlockSpec((r8, w), lambda b, be, nu: (last(b, nu), 0)),
                  pl.BlockSpec((1, d, f), lambda b, be, nu: (be[last(b, nu)], 0, 0)),
                  pl.BlockSpec((1, d, f), lambda b, be, nu: (be[last(b, nu)], 0, 0)),
                  pl.BlockSpec((1, f, d), lambda b, be, nu: (be[last(b, nu)], 0, 0))],
        out_specs=pl.BlockSpec((r8, w), lambda b, be, nu: (jnp.maximum(b - 1, 0), 0)),
        scratch_shapes=[pltpu.VMEM((d, f), BF16), pltpu.VMEM((d, f), BF16), pltpu.VMEM((f, d), BF16),
                        pltpu.VMEM((MOE_ROWS, d), F32)],
    )
    return pl.pallas_call(
        _gmm_kernel,
        grid_spec=grid_spec,
        out_shape=jax.ShapeDtypeStruct((nb * r8, w), jnp.uint32),
        compiler_params=_cparams(("arbitrary",)),
        name='moe_experts',
    )(blk_e, n_used, xs, wg, wu, wd)


def _combine_kernel(dest_ref, destn_ref, ys_ref, gate_ref, base_ref, g2_ref, lng_ref, lnb_ref, o_ref,
                    rows_scr, sem, *, tm, n_tiles):
    i = pl.program_id(0)
    slot_tok = TOP_K * tm
    slot_rows = slot_tok * TOK_SUB
    pitch = TOK_SUB

    def gather(d_ref, slot, t, j):
        return _row_copy(ys_ref, d_ref[t * TOP_K + j], rows_scr, slot * slot_tok + j * tm + t, sem.at[slot])

    def issue(d_ref, slot):
        def start(t, c):
            for j in range(TOP_K):
                gather(d_ref, slot, t, j).start(priority=j % 2)
            return c

        lax.fori_loop(0, tm, start, 0)

    def wait_slot(slot):
        pltpu.make_async_copy(ys_ref.at[pl.ds(0, slot_rows), :],
                              rows_scr.at[pl.ds(slot * slot_tok * pitch, slot_rows), :], sem.at[slot]).wait()

    def finish(slot):
        wait_slot(slot)
        d = o_ref.shape[1]
        w = V7X_LANES
        n_col = d // w
        tg = 32

        def group(gi, carry):
            g0 = pl.multiple_of(gi * tg, tg)
            grp = pl.ds(g0, tg)
            for tt in range(tg):
                for j in range(TOP_K):
                    gather(destn_ref, 1 - slot, g0 + tt, j).start(priority=j % 2)
            gates = gate_ref[grp, :]
            gb = [jnp.broadcast_to(gates[:, j:j + 1], (tg, w)) for j in range(TOP_K)]
            for s in range(TOK_SUB):
                acc_lo = acc_hi = None
                for j in range(TOP_K):
                    row0 = (slot * slot_tok + j * tm + g0) * pitch + s
                    lo, hi = _unpack_tile_rows(rows_scr[pl.ds(row0, tg, stride=pitch), :])
                    acc_lo = lo * gb[j] if acc_lo is None else acc_lo + lo * gb[j]
                    acc_hi = hi * gb[j] if acc_hi is None else acc_hi + hi * gb[j]
                for acc, c0 in ((acc_lo, s * w), (acc_hi, d // 2 + s * w)):
                    cols = slice(c0, c0 + w)
                    o_ref[grp, cols] = base_ref[grp, cols] + g2_ref[0, :, cols] * acc
            return carry

        lax.fori_loop(0, tm // tg, group, 0)
        total = o_ref[:, :w]
        for s in range(1, n_col):
            total = total + o_ref[:, s * w:(s + 1) * w]
        mu = jnp.sum(total, axis=-1, keepdims=True) / d
        sq = jnp.square(o_ref[:, :w] - mu)
        for s in range(1, n_col):
            sq = sq + jnp.square(o_ref[:, s * w:(s + 1) * w] - mu)
        rstd = lax.rsqrt(jnp.sum(sq, axis=-1, keepdims=True) / d + LN_EPS)
        for s in range(n_col):
            cols = slice(s * w, (s + 1) * w)
            o_ref[:, cols] = (o_ref[:, cols] - mu) * rstd * lng_ref[:, cols] + lnb_ref[:, cols]

    @pl.when(i == 0)
    def _():
        issue(dest_ref, 0)

    for slot in range(2):
        @pl.when(i % 2 == slot)
        def _():
            finish(slot)

            @pl.when(i == n_tiles - 1)
            def _():
                wait_slot(1 - slot)


def _combine(ys, dest_flat, gates, base, mods3, row_of_tile, ln_g, ln_b, name, tm=256):
    t, d = base.shape
    ep = gates.shape[1]
    n_tiles = t // tm
    kern = functools.partial(_combine_kernel, tm=tm, n_tiles=n_tiles)
    return pl.pallas_call(
        kern,
        grid=(n_tiles,),
        in_specs=[pl.BlockSpec((tm * TOP_K,), lambda i: (i,), memory_space=pltpu.SMEM),
                  pl.BlockSpec((tm * TOP_K,), lambda i: (jnp.minimum(i + 1, n_tiles - 1),), memory_space=pltpu.SMEM),
                  pl.BlockSpec(memory_space=pl.ANY),
                  pl.BlockSpec((tm, ep), lambda i: (i, 0)),
                  pl.BlockSpec((tm, d), lambda i: (i, 0)),
                  pl.BlockSpec((1, 1, d), lambda i: (row_of_tile(i, tm), 0, 5)),
                  _whole((1, d)), _whole((1, d))],
        out_specs=pl.BlockSpec((tm, d), lambda i: (i, 0)),
        out_shape=jax.ShapeDtypeStruct((t, d), F32),
        scratch_shapes=[pltpu.VMEM((2 * TOP_K * tm * TOK_SUB, V7X_LANES), jnp.uint32),
                        pltpu.SemaphoreType.DMA((2,))],
        compiler_params=_cparams(("arbitrary",)),
        name=name,
    )(dest_flat, dest_flat, ys, gates, base, mods3, ln_g.reshape(1, d), ln_b.reshape(1, d))


def _layer_pair(xp, xs, s_f, s_b, c, c_ctx, lp, alpha):
    bp, lp_len, d = xp.shape
    bs, ls_len, _ = xs.shape
    tp, ts = bp * lp_len, bs * ls_len
    assert lp_len % DFT_BLOCK == 0 and ls_len % DFT_BLOCK == 0 and DFT_BLOCK % (ls_len // DFT_BLOCK) == 0
    assert d == 2 * TOK_SUB * V7X_LANES
    n_exp = lp['router_w'].shape[1]
    r = MOE_ROWS

    mod_rows = 8 * ((1 + bs + 7) // 8)
    cond = jnp.zeros((mod_rows, d), F32).at[0].set(c_ctx).at[1:1 + bs].set(c)
    mods3 = _ada(cond, lp['w_ada'], lp['b_ada']).reshape(mod_rows, 1, 6 * d)
    row_ctx = lambda i, tile: 0
    row_lat = lambda i, tile: 1 + (i * tile) // ls_len

    w_in = lp['w_in'].astype(BF16)
    ep = max(n_exp, V7X_LANES)
    rw = jnp.zeros((d, ep), F32).at[:, :n_exp].set(lp['router_w'])
    rwh = rw.astype(BF16)
    rwl = (rw - rwh.astype(F32)).astype(BF16)
    rb = jnp.zeros((1, ep), F32).at[0, :n_exp].set(lp['router_b'])
    out_wts = (lp['w_out'].astype(BF16), lp['sh_w_gate'].astype(BF16), lp['sh_w_up'].astype(BF16),
               lp['sh_w_down'].astype(BF16), rwh, rwl, lp['ln1_g'], lp['ln1_b'], rb)
    cw = lp['hy_bias'].shape[1]
    hy_cols = (HY_ORDER + 1) * cw
    lg = jnp.stack([jax.nn.log_sigmoid(lp['ret_decay_f'].astype(F32)),
                    jax.nn.log_sigmoid(lp['ret_decay_b'].astype(F32))])
    dk = s_f.shape[-1]
    zero_state = jnp.zeros((bp, RET_HEADS, dk, dk), F32)

    def mixers(x2, n_seq, L, row_of_tile, s0f, s0b, cnt0, latent, tag):
        proj = _proj(x2, mods3, row_of_tile, w_in, 'proj_' + tag, tm=min(PROJ_ROWS, L if latent else x2.shape[0]))
        y_hy = _hyena(proj, n_seq, L, lp, cw, tag)
        y_ret, nf, nb_ = _retention(proj, n_seq, L, hy_cols, lg, s0f, s0b, latent, 'retention_' + tag)
        routed = _out_stage(y_hy, y_ret, x2, mods3, row_of_tile, out_wts, cnt0, alpha, n_exp, 'out_' + tag)
        return routed, nf, nb_

    (base_p, h2t_p, idx_p, gate_p, rank_p, cnt_p), nf, nb_ = mixers(
        xp.reshape(tp, d), bp, lp_len, row_ctx, zero_state, zero_state, jnp.zeros((1, ep), F32), False, 'ctx')
    (base_s, h2t_s, idx_s, gate_s, rank_s, cnt_all), _, _ = mixers(
        xs.reshape(ts, d), bs, ls_len, row_lat, s_f, s_b, cnt_p, True, 'lat')

    t = tp + ts
    counts = cnt_all[0, :n_exp].astype(jnp.int32)
    padded = (counts + r - 1) // r * r
    pend = jnp.cumsum(padded)
    pstart = pend - padded
    dest_p = (pstart[idx_p[:, :TOP_K]] + rank_p[:, :TOP_K]).reshape(tp * TOP_K)
    dest_s = (pstart[idx_s[:, :TOP_K]] + rank_s[:, :TOP_K]).reshape(ts * TOP_K)
    dest = jnp.concatenate([dest_p, dest_s])
    nb = (t * TOP_K) // r + n_exp
    blk_start = jnp.arange(nb, dtype=jnp.int32) * r
    blk_e = jnp.minimum(jnp.sum(pend[None, :] <= blk_start[:, None], axis=1), n_exp - 1).astype(jnp.int32)
    n_used = (pend[-1:] // r).astype(jnp.int32)
    zpos = jnp.concatenate([pstart + counts, padded - counts, n_used]).astype(jnp.int32)
    xsort = _dispatch(h2t_p, h2t_s, dest, zpos, nb, n_exp)
    ysort = _gmm(xsort, blk_e, n_used, lp['exp_w_gate'], lp['exp_w_up'], lp['exp_w_down'], nb)
    y_p = _combine(ysort, dest_p, gate_p, base_p, mods3, row_ctx, lp['ln2_g'], lp['ln2_b'], 'moe_combine_ctx')
    y_s = _combine(ysort, dest_s, gate_s, base_s, mods3, row_lat, lp['ln2_g'], lp['ln2_b'], 'moe_combine_lat')
    return y_p.reshape(bp, lp_len, d), y_s.reshape(bs, ls_len, d), nf, nb_


def kernel(x_prompt, x_sample, state_ret_fwd, state_ret_bwd, c, c_ctx, w_in, hy_conv_w, hy_conv_b, hy_ffn_w1, hy_ffn_b1, hy_ffn_w2, hy_ffn_b2, hy_ffn_w3, hy_sin_freq, hy_bias, ret_decay_f, ret_decay_b, w_out, w_ada, b_ada, ln1_g, ln1_b, ln2_g, ln2_b, router_w, router_b, exp_w_gate, exp_w_up, exp_w_down, sh_w_gate, sh_w_up, sh_w_down):
    depth = w_in.shape[0]
    alpha = (2.0 * depth) ** 0.25
    params = dict(w_in=w_in, hy_conv_w=hy_conv_w, hy_conv_b=hy_conv_b, hy_ffn_w1=hy_ffn_w1, hy_ffn_b1=hy_ffn_b1,
                  hy_ffn_w2=hy_ffn_w2, hy_ffn_b2=hy_ffn_b2, hy_ffn_w3=hy_ffn_w3, hy_sin_freq=hy_sin_freq,
                  hy_bias=hy_bias, ret_decay_f=ret_decay_f, ret_decay_b=ret_decay_b, w_out=w_out, w_ada=w_ada,
                  b_ada=b_ada, ln1_g=ln1_g, ln1_b=ln1_b, ln2_g=ln2_g, ln2_b=ln2_b, router_w=router_w,
                  router_b=router_b, exp_w_gate=exp_w_gate, exp_w_up=exp_w_up, exp_w_down=exp_w_down,
                  sh_w_gate=sh_w_gate, sh_w_up=sh_w_up, sh_w_down=sh_w_down)
    y_p, y_s = x_prompt, x_sample
    new_f, new_b = [], []
    for l in range(depth):
        lp = {k: v[l] for k, v in params.items()}
        y_p, y_s, s_f, s_b = _layer_pair(y_p, y_s, state_ret_fwd[:, l], state_ret_bwd[:, l], c, c_ctx, lp, alpha)
        new_f.append(s_f.astype(x_prompt.dtype))
        new_b.append(s_b.astype(x_prompt.dtype))
    return (y_p, y_s, jnp.stack(new_f, axis=1), jnp.stack(new_b, axis=1))
```
